```python
import numpy as np
import jax
import jax.numpy as jnp
from jax import lax

D_MODEL = 4096
BATCH = 4
SEQ = 2048
DEPTH = 4
DEC_BATCH = 8
DEC_SEQ = 4
PAST_LEN = 8192
PAGE_SIZE = 128

D_MIX = D_MODEL
D_GRP = D_MIX // 4
D_A = D_GRP
D_B = D_GRP
D_C = D_GRP
D_D = D_GRP
POOL_WINDOWS = (2, 4, 8, 16)
N_POOL = len(POOL_WINDOWS)
POOL_CH = D_A // N_POOL
POOL_BUF = max(POOL_WINDOWS) - 1
CONV_B_WIDTH = 31
CONV_C_WIDTH = 3
HEAD_DIM = 128
N_HEADS_D = D_D // HEAD_DIM
DILATED_CFG = ((128, 1), (512, 4), (2048, 16))
WIN_MAX = max(w for w, _ in DILATED_CFG)
BLK = 128
D_FF = ((8 * D_MODEL // 3 + 255) // 256) * 256
N_IN = D_A + 2 * D_B + 3 * D_C + 3 * D_D
EPS = 1e-6
NEG = -1e30

kernel_name = 'hybrid_pool_conv_dilated_decoder'


def rmsnorm(x, g):
    xf = x.astype(jnp.float32)
    y = xf * lax.rsqrt(jnp.mean(xf * xf, axis=-1, keepdims=True) + EPS)
    return (y * g.astype(jnp.float32)).astype(x.dtype)


def layernorm(x, g, b):
    xf = x.astype(jnp.float32)
    mu = jnp.mean(xf, axis=-1, keepdims=True)
    xc = xf - mu
    var = jnp.mean(xc * xc, axis=-1, keepdims=True)
    y = xc * lax.rsqrt(var + EPS) * g.astype(jnp.float32) + b.astype(jnp.float32)
    return y.astype(x.dtype)


def swiglu(h, wg, wu, wd):
    return (jax.nn.silu(h @ wg) * (h @ wu)) @ wd


def causal_dwconv(full, w):
    c = full.shape[-1]
    return lax.conv_general_dilated(full, w[:, None, :], window_strides=(1,), padding='VALID',
                                    dimension_numbers=('NWC', 'WIO', 'NWC'), feature_group_count=c)


def pool_mixer(u_full, n_prev, w_pool, scale):
    n, length, _ = u_full.shape
    t = length - n_prev
    uf = u_full.astype(jnp.float32).reshape(n, length, N_POOL, POOL_CH)
    cs = jnp.concatenate([jnp.zeros_like(uf[:, :1]), jnp.cumsum(uf, axis=1)], axis=1)
    hi = np.arange(n_prev + 1, length + 1)
    means = []
    for g, w in enumerate(POOL_WINDOWS):
        lo = np.maximum(hi - w, 0)
        cnt = (hi - lo).astype(np.float32)
        csg = cs[:, :, g]
        means.append((jnp.take(csg, hi, axis=1) - jnp.take(csg, lo, axis=1)) / cnt[None, :, None])
    pooled = jnp.stack(means, axis=2) - uf[:, n_prev:]
    y = jnp.einsum('ntgc,gcd->ntgd', pooled.astype(u_full.dtype), w_pool)
    return y.reshape(n, t, D_A) * scale


def conformer_conv(val, gate, buf, w_dw, b_dw, ln_g, ln_b):
    u = val * jax.nn.sigmoid(gate)
    full = jnp.concatenate([buf, u], axis=1)
    y = causal_dwconv(full, w_dw) + b_dw
    y = jax.nn.silu(layernorm(y, ln_g, ln_b))
    return y, full[:, -(CONV_B_WIDTH - 1):]


def short_conv(b_gate, c_gate, h, buf, w_dw):
    g = c_gate * h
    full = jnp.concatenate([buf, g], axis=1)
    y = b_gate * causal_dwconv(full, w_dw)
    return y, full[:, -(CONV_C_WIDTH - 1):]


def dilated_branch_prompt(q, k, v, window, dil):
    b, s, h, hd = q.shape
    j_max = window // dil
    length = s // dil
    pad = (-length) % BLK
    nb = (length + pad) // BLK

    def fold(t, p):
        t = t.reshape(b, length, dil, h, hd).transpose(0, 2, 1, 3, 4).reshape(b * dil, length, h, hd)
        return jnp.pad(t, ((0, 0), (p, 0), (0, 0), (0, 0)))

    def unfold(t):
        t = t[:, pad:]
        t = t.reshape((b, dil, length) + t.shape[2:])
        t = jnp.swapaxes(t, 1, 2)
        return t.reshape((b, s) + t.shape[3:])

    qb = fold(q, pad).reshape(b * dil, nb, BLK, h, hd)
    kf = fold(k, pad + BLK).reshape(b * dil, nb + 1, BLK, h, hd)
    vf = fold(v, pad + BLK).reshape(b * dil, nb + 1, BLK, h, hd)
    kb = jnp.concatenate([kf[:, :-1], kf[:, 1:]], axis=2)
    vb = jnp.concatenate([vf[:, :-1], vf[:, 1:]], axis=2)
    sc = jnp.einsum('nbqhd,nbkhd->nbhqk', qb, kb).astype(jnp.float32)
    qi = np.arange(nb)[:, None, None] * BLK + np.arange(BLK)[None, :, None]
    ki = (np.arange(nb)[:, None, None] - 1) * BLK + np.arange(2 * BLK)[None, None, :]
    dist = qi - ki
    valid = (dist >= 0) & (dist <= j_max) & (ki >= pad)
    sc = jnp.where(valid[None, :, None], sc, NEG)
    m = jnp.max(sc, axis=-1)
    p = jnp.exp(sc - m[..., None])
    z = jnp.sum(p, axis=-1)
    u = jnp.einsum('nbhqk,nbkhd->nbqhd', p, vb.astype(jnp.float32)).reshape(b * dil, nb * BLK, h, hd)
    m = m.transpose(0, 1, 3, 2).reshape(b * dil, nb * BLK, h)
    z = z.transpose(0, 1, 3, 2).reshape(b * dil, nb * BLK, h)
    return unfold(u), unfold(m), unfold(z)


def dilated_branch_sample(q, kc, vc, window, dil):
    t = q.shape[1]
    lc = kc.shape[1] - t
    j_max = window // dil
    idx = lc + np.arange(t)[:, None] - dil * np.arange(j_max + 1)[None, :]
    valid = idx >= 0
    idx = np.maximum(idx, 0)
    kg = kc[:, idx]
    vg = vc[:, idx]
    sc = jnp.einsum('nthd,ntjhd->nthj', q, kg).astype(jnp.float32)
    sc = jnp.where(valid[None, :, None, :], sc, NEG)
    m = jnp.max(sc, axis=-1)
    p = jnp.exp(sc - m[..., None])
    z = jnp.sum(p, axis=-1)
    u = jnp.einsum('nthj,ntjhd->nthd', p, vg.astype(jnp.float32))
    return u, m, z


def combine_branches(parts):
    ms = jnp.stack([pm for _, pm, _ in parts])
    mx = jnp.max(ms, axis=0)
    wts = jnp.exp(ms - mx)
    den = sum(wts[i] * parts[i][2] for i in range(len(parts)))
    num = sum(wts[i][..., None] * parts[i][0] for i in range(len(parts)))
    return num / den[..., None]


def token_mixing(h, lp, bufs):
    n, t, _ = h.shape
    zproj = h @ lp['w_in']
    cuts = np.cumsum([D_A, D_B, D_B, D_C, D_C, D_C, D_D, D_D])
    ua, b_val, b_gate, c_b, c_c, c_h, q, k, v = jnp.split(zproj, cuts, axis=-1)
    if bufs is None:
        pool_full, n_prev = ua, 0
        buf_b = jnp.zeros((n, CONV_B_WIDTH - 1, D_B), h.dtype)
        buf_c = jnp.zeros((n, CONV_C_WIDTH - 1, D_C), h.dtype)
    else:
        pool_buf, buf_b, buf_c, k_cache, v_cache = bufs
        pool_full, n_prev = jnp.concatenate([pool_buf, ua], axis=1), POOL_BUF
    y_a = pool_mixer(pool_full, n_prev, lp['w_pool'], lp['pool_scale'])
    new_pool = pool_full[:, -POOL_BUF:]
    y_b, new_b = conformer_conv(b_val, b_gate, buf_b, lp['w_dw_b'], lp['b_dw_b'], lp['ln_b_g'], lp['ln_b_b'])
    y_c, new_c = short_conv(c_b, c_c, c_h, buf_c, lp['w_dw_c'])
    q = rmsnorm(q.reshape(n, t, N_HEADS_D, HEAD_DIM), lp['q_norm_g']) * (HEAD_DIM ** -0.5)
    k = rmsnorm(k.reshape(n, t, N_HEADS_D, HEAD_DIM), lp['k_norm_g'])
    v = v.reshape(n, t, N_HEADS_D, HEAD_DIM)
    if bufs is None:
        parts = [dilated_branch_prompt(q, k, v, w, d) for (w, d) in DILATED_CFG]
        keep = min(WIN_MAX, t)
        new_k, new_v = k[:, -keep:], v[:, -keep:]
    else:
        kc = jnp.concatenate([k_cache, k], axis=1)
        vc = jnp.concatenate([v_cache, v], axis=1)
        parts = [dilated_branch_sample(q, kc, vc, w, d) for (w, d) in DILATED_CFG]
        new_k, new_v = k, v
    y_d = combine_branches(parts).astype(h.dtype).reshape(n, t, D_D)
    out = jnp.concatenate([y_a, y_b, y_c, y_d], axis=-1) @ lp['w_out']
    return out, (new_pool, new_b, new_c, new_k, new_v)


def layer(x, lp, bufs):
    x = x + 0.5 * swiglu(rmsnorm(x, lp['g_ffn1']), lp['w_ffn1_gate'], lp['w_ffn1_up'], lp['w_ffn1_down'])
    mix, st = token_mixing(rmsnorm(x, lp['g_mix']), lp, bufs)
    x = x + mix
    x = x + 0.5 * swiglu(rmsnorm(x, lp['g_ffn2']), lp['w_ffn2_gate'], lp['w_ffn2_up'], lp['w_ffn2_down'])
    return x, st


def setup_inputs(seed: int = 0) -> dict:
    key = jax.random.key(seed)
    ks = iter(jax.random.split(key, 32))

    def nrm(shape, scale):
        return jax.random.normal(next(ks), shape, jnp.float32) * scale

    def gain(shape):
        return 1.0 + nrm(shape, 0.05)

    n_cache = min(WIN_MAX, PAST_LEN)
    return {
        'x_prompt': nrm((BATCH, SEQ, D_MODEL), 1.0),
        'x_sample': nrm((DEC_BATCH, DEC_SEQ, D_MODEL), 1.0),
        'state_pool': nrm((DEPTH, DEC_BATCH, POOL_BUF, D_A), 1.0),
        'state_conv_b': nrm((DEPTH, DEC_BATCH, CONV_B_WIDTH - 1, D_B), 0.5),
        'state_conv_c': nrm((DEPTH, DEC_BATCH, CONV_C_WIDTH - 1, D_C), 1.0),
        'cache_k': nrm((DEPTH, DEC_BATCH, n_cache, N_HEADS_D, HEAD_DIM), 1.0),
        'cache_v': nrm((DEPTH, DEC_BATCH, n_cache, N_HEADS_D, HEAD_DIM), 1.0),
        'g_ffn1': gain((DEPTH, D_MODEL)),
        'w_ffn1_gate': nrm((DEPTH, D_MODEL, D_FF), D_MODEL ** -0.5),
        'w_ffn1_up': nrm((DEPTH, D_MODEL, D_FF), D_MODEL ** -0.5),
        'w_ffn1_down': nrm((DEPTH, D_FF, D_MODEL), D_FF ** -0.5),
        'g_mix': gain((DEPTH, D_MODEL)),
        'w_in': nrm((DEPTH, D_MODEL, N_IN), D_MODEL ** -0.5),
        'w_pool': nrm((DEPTH, N_POOL, POOL_CH, POOL_CH), POOL_CH ** -0.5),
        'pool_scale': 1.0 + nrm((DEPTH, D_A), 0.1),
        'w_dw_b': nrm((DEPTH, CONV_B_WIDTH, D_B), CONV_B_WIDTH ** -0.5),
        'b_dw_b': nrm((DEPTH, D_B), 0.02),
        'ln_b_g': gain((DEPTH, D_B)),
        'ln_b_b': nrm((DEPTH, D_B), 0.02),
        'w_dw_c': nrm((DEPTH, CONV_C_WIDTH, D_C), CONV_C_WIDTH ** -0.5),
        'q_norm_g': gain((DEPTH, HEAD_DIM)),
        'k_norm_g': gain((DEPTH, HEAD_DIM)),
        'w_out': nrm((DEPTH, D_MIX, D_MODEL), D_MIX ** -0.5),
        'g_ffn2': gain((DEPTH, D_MODEL)),
        'w_ffn2_gate': nrm((DEPTH, D_MODEL, D_FF), D_MODEL ** -0.5),
        'w_ffn2_up': nrm((DEPTH, D_MODEL, D_FF), D_MODEL ** -0.5),
        'w_ffn2_down': nrm((DEPTH, D_FF, D_MODEL), D_FF ** -0.5),
    }


def reference(x_prompt, x_sample, state_pool, state_conv_b, state_conv_c, cache_k, cache_v,
              g_ffn1, w_ffn1_gate, w_ffn1_up, w_ffn1_down, g_mix, w_in, w_pool, pool_scale,
              w_dw_b, b_dw_b, ln_b_g, ln_b_b, w_dw_c, q_norm_g, k_norm_g, w_out,
              g_ffn2, w_ffn2_gate, w_ffn2_up, w_ffn2_down):
    yp, ys = x_prompt, x_sample
    outs_p, outs_s = [], []
    for l in range(DEPTH):
        lp = {
            'g_ffn1': g_ffn1[l], 'w_ffn1_gate': w_ffn1_gate[l], 'w_ffn1_up': w_ffn1_up[l],
            'w_ffn1_down': w_ffn1_down[l], 'g_mix': g_mix[l], 'w_in': w_in[l], 'w_pool': w_pool[l],
            'pool_scale': pool_scale[l], 'w_dw_b': w_dw_b[l], 'b_dw_b': b_dw_b[l], 'ln_b_g': ln_b_g[l],
            'ln_b_b': ln_b_b[l], 'w_dw_c': w_dw_c[l], 'q_norm_g': q_norm_g[l], 'k_norm_g': k_norm_g[l],
            'w_out': w_out[l], 'g_ffn2': g_ffn2[l], 'w_ffn2_gate': w_ffn2_gate[l],
            'w_ffn2_up': w_ffn2_up[l], 'w_ffn2_down': w_ffn2_down[l],
        }
        yp, sp = layer(yp, lp, None)
        ys, ss = layer(ys, lp, (state_pool[l], state_conv_b[l], state_conv_c[l], cache_k[l], cache_v[l]))
        outs_p.append(sp)
        outs_s.append(ss)

    def stk(outs, i):
        return jnp.stack([o[i] for o in outs])

    return (yp, ys, stk(outs_p, 0), stk(outs_s, 0), stk(outs_p, 1), stk(outs_s, 1),
            stk(outs_p, 2), stk(outs_s, 2), stk(outs_p, 3), stk(outs_p, 4), stk(outs_s, 3), stk(outs_s, 4))
```

```python
import functools

import jax
import jax.numpy as jnp
from jax import lax
from jax.experimental import pallas as pl
from jax.experimental.pallas import tpu as pltpu

EPS = 1e-6
NEG = -1e30
LANES = 128
HEAD_DIM = 128
POOL_WINDOWS = (2, 4, 8, 16)
POOL_BUF = max(POOL_WINDOWS) - 1
CONV_B_WIDTH = 31
CONV_C_WIDTH = 3
DILATED_CFG = ((128, 1), (512, 4), (2048, 16))
ATT_BLK = 128
VMEM_LIMIT = 60 * 1024 * 1024

F32 = jnp.float32
BF16 = jnp.bfloat16


def _cparams(n_axes):
    return pltpu.CompilerParams(dimension_semantics=("arbitrary",) * n_axes,
                                vmem_limit_bytes=VMEM_LIMIT)


def _lane_fold(v):
    n = v.shape[-1] // LANES
    out = v[:, 0:LANES]
    for k in range(1, n):
        out = out + v[:, k * LANES:(k + 1) * LANES]
    return out


def _row_rms_scale(ssq, d_model):
    return lax.rsqrt(jnp.sum(ssq, axis=-1, keepdims=True) * (1.0 / d_model) + EPS)


def _prep_kernel(x_ref, g_ref, xg_ref, ssq_ref):
    x = x_ref[...]
    xg_ref[...] = (x * g_ref[...]).astype(BF16)
    ssq_ref[...] = _lane_fold(x * x)


def _prep(x, g_row, tm):
    m, d = x.shape
    return pl.pallas_call(
        _prep_kernel,
        grid=(m // tm,),
        in_specs=[pl.BlockSpec((tm, d), lambda i: (i, 0)),
                  pl.BlockSpec((1, d), lambda i: (0, 0))],
        out_specs=[pl.BlockSpec((tm, d), lambda i: (i, 0)),
                   pl.BlockSpec((tm, LANES), lambda i: (i, 0))],
        out_shape=[jax.ShapeDtypeStruct((m, d), BF16),
                   jax.ShapeDtypeStruct((m, LANES), F32)],
        compiler_params=_cparams(1),
        name="norm_prep",
    )(x, g_row)


def _up_kernel(l_ref, xg_ref, ssq_ref, *rest, n_w, d_model):
    w_refs, o_ref = rest[:n_w], rest[n_w]
    r = _row_rms_scale(ssq_ref[...], d_model)
    xg = xg_ref[...]
    outs = [jnp.dot(xg, w[...].astype(BF16), preferred_element_type=F32) * r for w in w_refs]
    o = outs[0] if n_w == 1 else jax.nn.silu(outs[0]) * outs[1]
    o_ref[...] = o.astype(o_ref.dtype)


def _normed_matmul(lidx, xg, ssq, ws, *, tm, tn, out_dtype, name):
    m, k = xg.shape
    n = ws[0].shape[-1]
    grid_spec = pltpu.PrefetchScalarGridSpec(
        num_scalar_prefetch=1,
        grid=(m // tm, n // tn),
        in_specs=[pl.BlockSpec((tm, k), lambda i, j, l: (i, 0)),
                  pl.BlockSpec((tm, LANES), lambda i, j, l: (i, 0))]
                 + [pl.BlockSpec((None, k, tn), lambda i, j, l: (l[0], 0, j)) for _ in ws],
        out_specs=pl.BlockSpec((tm, tn), lambda i, j, l: (i, j)),
    )
    return pl.pallas_call(
        functools.partial(_up_kernel, n_w=len(ws), d_model=k),
        grid_spec=grid_spec,
        out_shape=jax.ShapeDtypeStruct((m, n), out_dtype),
        compiler_params=_cparams(2),
        name=name,
    )(lidx, xg, ssq, *ws)


def _down_kernel(l_ref, *refs, n_a, scale):
    a_refs, w_refs = refs[:n_a], refs[n_a:2 * n_a]
    res_ref, g_ref, x_out, xg_out, ssq_out = refs[2 * n_a:]
    acc = None
    for a, w in zip(a_refs, w_refs):
        part = jnp.dot(a[...], w[...].astype(BF16), preferred_element_type=F32)
        acc = part if acc is None else acc + part
    xn = res_ref[...] + scale * acc
    x_out[...] = xn
    xg_out[...] = (xn * g_ref[...]).astype(BF16)
    part_ssq = _lane_fold(xn * xn)

    @pl.when(pl.program_id(1) == 0)
    def _():
        ssq_out[...] = part_ssq

    @pl.when(pl.program_id(1) > 0)
    def _():
        ssq_out[...] += part_ssq


def _residual_matmul(lidx, a_list, w, row_blocks, res, g_next, *, scale, tm, tn, name):
    m = res.shape[0]
    n = w.shape[-1]
    in_specs = [pl.BlockSpec((tm, a.shape[1]), lambda i, j, l: (i, 0), pipeline_mode=pl.Buffered(1))
                for a in a_list]
    in_specs += [pl.BlockSpec((None, a.shape[1], tn), functools.partial(lambda i, j, l, rb: (l[0], rb, j), rb=rb))
                 for a, rb in zip(a_list, row_blocks)]
    in_specs += [pl.BlockSpec((tm, tn), lambda i, j, l: (i, j)),
                 pl.BlockSpec((None, 1, tn), lambda i, j, l: (l[1], 0, j))]
    grid_spec = pltpu.PrefetchScalarGridSpec(
        num_scalar_prefetch=1,
        grid=(m // tm, n // tn),
        in_specs=in_specs,
        out_specs=[pl.BlockSpec((tm, tn), lambda i, j, l: (i, j)),
                   pl.BlockSpec((tm, tn), lambda i, j, l: (i, j)),
                   pl.BlockSpec((tm, LANES), lambda i, j, l: (i, 0))],
    )
    return pl.pallas_call(
        functools.partial(_down_kernel, n_a=len(a_list), scale=scale),
        grid_spec=grid_spec,
        out_shape=[jax.ShapeDtypeStruct((m, n), F32),
                   jax.ShapeDtypeStruct((m, n), BF16),
                   jax.ShapeDtypeStruct((m, LANES), F32)],
        compiler_params=_cparams(2),
        name=name,
    )(lidx, *a_list, *([w] * len(a_list)), res, g_next)


HIST_A = 16
HIST_B = 32
HIST_C = 8


def _mix_abc_kernel(l_ref, *refs, tt, tr, t_valid, n_prev, n_chunks, has_state, dg):
    if has_state:
        (z_ref, sp_ref, sb_ref, sc_ref, wpool_ref, pscale_ref, wdwb_ref, bdwb_ref, lng_ref, lnb_ref, wdwc_ref,
         y_ref, np_ref, nb_ref, nc_ref, full_a, full_b, full_c, pooled_s) = refs
    else:
        (z_ref, wpool_ref, pscale_ref, wdwb_ref, bdwb_ref, lng_ref, lnb_ref, wdwc_ref,
         y_ref, np_ref, nb_ref, nc_ref, full_a, full_b, full_c, pooled_s) = refs
    c = pl.program_id(1)

    @pl.when(c == 0)
    def _():
        full_a[0:HIST_A, :] = jnp.zeros((HIST_A, dg), F32)
        full_b[0:HIST_B, :] = jnp.zeros((HIST_B, dg), F32)
        full_c[0:HIST_C, :] = jnp.zeros((HIST_C, dg), F32)
        if has_state:
            full_a[HIST_A - POOL_BUF:HIST_A, :] = sp_ref[...]
            full_b[HIST_B - (CONV_B_WIDTH - 1):HIST_B, :] = sb_ref[...]
            full_c[HIST_C - (CONV_C_WIDTH - 1):HIST_C, :] = sc_ref[...]

    full_a[HIST_A:HIST_A + tt, :] = z_ref[:, 0:dg]
    full_b[HIST_B:HIST_B + tt, :] = z_ref[:, dg:2 * dg] * jax.nn.sigmoid(z_ref[:, 2 * dg:3 * dg])
    full_c[HIST_C:HIST_C + tt, :] = z_ref[:, 4 * dg:5 * dg] * z_ref[:, 5 * dg:6 * dg]

    pch = dg // len(POOL_WINDOWS)
    for r0 in range(0, tt, tr):
        t_glob = c * tt + r0 + lax.broadcasted_iota(jnp.int32, (tr, 1), 0)
        for g, w in enumerate(POOL_WINDOWS):
            cols = slice(g * pch, (g + 1) * pch)
            cur = full_a[HIST_A + r0:HIST_A + r0 + tr, cols]
            acc = cur
            for s in range(1, w):
                acc = acc + full_a[HIST_A + r0 - s:HIST_A + r0 - s + tr, cols]
            cnt = jnp.minimum(w, n_prev + t_glob + 1).astype(F32)
            pooled_s[r0:r0 + tr, cols] = acc / cnt - cur

        acc = jnp.zeros((tr, dg), F32) + bdwb_ref[...]
        off = HIST_B - (CONV_B_WIDTH - 1) + r0
        for j in range(CONV_B_WIDTH):
            acc = acc + wdwb_ref[j:j + 1, :] * full_b[off + j:off + j + tr, :]
        mu = jnp.mean(acc, axis=-1, keepdims=True)
        xc = acc - mu
        var = jnp.mean(xc * xc, axis=-1, keepdims=True)
        yn = xc * lax.rsqrt(var + EPS) * lng_ref[...] + lnb_ref[...]
        y_ref[r0:r0 + tr, dg:2 * dg] = jax.nn.silu(yn).astype(y_ref.dtype)

        acc = jnp.zeros((tr, dg), F32)
        off = HIST_C - (CONV_C_WIDTH - 1) + r0
        for j in range(CONV_C_WIDTH):
            acc = acc + wdwc_ref[j:j + 1, :] * full_c[off + j:off + j + tr, :]
        y_ref[r0:r0 + tr, 2 * dg:3 * dg] = (z_ref[r0:r0 + tr, 3 * dg:4 * dg] * acc).astype(y_ref.dtype)

    for g in range(len(POOL_WINDOWS)):
        cols = slice(g * pch, (g + 1) * pch)
        ya = jnp.dot(pooled_s[:, cols].astype(BF16), wpool_ref[g].astype(BF16), preferred_element_type=F32)
        y_ref[:, cols] = (ya * pscale_ref[:, cols]).astype(y_ref.dtype)

    @pl.when(c == n_chunks - 1)
    def _():
        np_ref[...] = full_a[HIST_A + t_valid - POOL_BUF:HIST_A + t_valid, :]
        nb_ref[...] = full_b[HIST_B + t_valid - (CONV_B_WIDTH - 1):HIST_B + t_valid, :]
        nc_ref[...] = full_c[HIST_C + t_valid - (CONV_C_WIDTH - 1):HIST_C + t_valid, :]

    if n_chunks > 1:
        @pl.when(c < n_chunks - 1)
        def _():
            full_a[0:HIST_A, :] = full_a[tt:tt + HIST_A, :]
            full_b[0:HIST_B, :] = full_b[tt:tt + HIST_B, :]
            full_c[0:HIST_C, :] = full_c[tt:tt + HIST_C, :]


def _mix_abc(lidx, z, states, wts, *, tt, tr, t_valid, n_prev, out_dtype, name):
    b, t, _ = z.shape
    w_pool, pool_scale, w_dw_b, b_dw_b, ln_g, ln_b, w_dw_c = wts
    dg = pool_scale.shape[-1]
    n_chunks = t // tt
    has_state = states is not None

    def lsel(*tail):
        return lambda bi, ci, l: (l[0],) + tail

    in_specs = [pl.BlockSpec((None, tt, 6 * dg), lambda bi, ci, l: (bi, ci, 0))]
    args = [z]
    if has_state:
        for s in states:
            in_specs.append(pl.BlockSpec((None, None) + s.shape[2:], lambda bi, ci, l: (l[0], bi, 0, 0)))
            args.append(s)
    in_specs += [
        pl.BlockSpec((None,) + w_pool.shape[1:], lsel(0, 0, 0)),
        pl.BlockSpec((None, 1, dg), lsel(0, 0)),
        pl.BlockSpec((None, CONV_B_WIDTH, dg), lsel(0, 0)),
        pl.BlockSpec((None, 1, dg), lsel(0, 0)),
        pl.BlockSpec((None, 1, dg), lsel(0, 0)),
        pl.BlockSpec((None, 1, dg), lsel(0, 0)),
        pl.BlockSpec((None, CONV_C_WIDTH, dg), lsel(0, 0)),
    ]
    args += [w_pool, pool_scale, w_dw_b, b_dw_b, ln_g, ln_b, w_dw_c]
    grid_spec = pltpu.PrefetchScalarGridSpec(
        num_scalar_prefetch=1,
        grid=(b, n_chunks),
        in_specs=in_specs,
        out_specs=[pl.BlockSpec((None, tt, 3 * dg), lambda bi, ci, l: (bi, ci, 0)),
                   pl.BlockSpec((None, POOL_BUF, dg), lambda bi, ci, l: (bi, 0, 0)),
                   pl.BlockSpec((None, CONV_B_WIDTH - 1, dg), lambda bi, ci, l: (bi, 0, 0)),
                   pl.BlockSpec((None, CONV_C_WIDTH - 1, dg), lambda bi, ci, l: (bi, 0, 0))],
        scratch_shapes=[pltpu.VMEM((HIST_A + tt, dg), F32),
                        pltpu.VMEM((HIST_B + tt, dg), F32),
                        pltpu.VMEM((HIST_C + tt, dg), F32),
                        pltpu.VMEM((tt, dg), F32)],
    )
    return pl.pallas_call(
        functools.partial(_mix_abc_kernel, tt=tt, tr=tr, t_valid=t_valid, n_prev=n_prev,
                          n_chunks=n_chunks, has_state=has_state, dg=dg),
        grid_spec=grid_spec,
        out_shape=[jax.ShapeDtypeStruct((b, t, 3 * dg), out_dtype),
                   jax.ShapeDtypeStruct((b, POOL_BUF, dg), F32),
                   jax.ShapeDtypeStruct((b, CONV_B_WIDTH - 1, dg), F32),
                   jax.ShapeDtypeStruct((b, CONV_C_WIDTH - 1, dg), F32)],
        compiler_params=_cparams(2),
        name=name,
    )(lidx, *args)


def _head_rmsnorm(x, g):
    return x * lax.rsqrt(jnp.mean(x * x, axis=-1, keepdims=True) + EPS) * g


def _dot_nt(a, b):
    return lax.dot_general(a, b, (((1,), (1,)), ((), ())), preferred_element_type=F32)


def _attn_prompt_kernel(l_ref, q_ref, k_ref, v_ref, gq_ref, gk_ref, y_ref, ko_ref, qn_s, m_s, l_s, acc_s):
    t = q_ref.shape[0]
    blk = ATT_BLK
    qn_s[...] = _head_rmsnorm(q_ref[...], gq_ref[...]) * (HEAD_DIM ** -0.5)
    ko_ref[...] = _head_rmsnorm(k_ref[...], gk_ref[...])
    m_s[...] = jnp.full(m_s.shape, NEG, F32)
    l_s[...] = jnp.zeros(l_s.shape, F32)
    acc_s[...] = jnp.zeros(acc_s.shape, F32)

    row = lax.broadcasted_iota(jnp.int32, (blk, blk), 0)
    col = lax.broadcasted_iota(jnp.int32, (blk, blk), 1)

    def rows(start, dil):
        return pl.ds(start, blk, stride=dil) if dil > 1 else pl.ds(start, blk)

    def update(start, dil, with_prev):
        idx = rows(start, dil)
        qb = qn_s[idx, :].astype(BF16)
        s = jnp.where(col <= row, _dot_nt(qb, ko_ref[idx, :].astype(BF16)), NEG)
        mx = jnp.max(s, axis=-1, keepdims=True)
        if with_prev:
            pidx = rows(start - blk * dil, dil)
            sp = jnp.where(col >= row, _dot_nt(qb, ko_ref[pidx, :].astype(BF16)), NEG)
            mx = jnp.maximum(mx, jnp.max(sp, axis=-1, keepdims=True))
        m_old = m_s[idx, :]
        m_new = jnp.maximum(m_old, mx)
        alpha = jnp.exp(m_old - m_new)
        p = jnp.exp(s - m_new)
        lsum = jnp.sum(p, axis=-1, keepdims=True)
        pv = jnp.dot(p.astype(BF16), v_ref[idx, :].astype(BF16), preferred_element_type=F32)
        if with_prev:
            pp = jnp.exp(sp - m_new)
            lsum = lsum + jnp.sum(pp, axis=-1, keepdims=True)
            pv = pv + jnp.dot(pp.astype(BF16), v_ref[pidx, :].astype(BF16), preferred_element_type=F32)
        m_s[idx, :] = m_new
        l_s[idx, :] = alpha * l_s[idx, :] + lsum
        acc_s[idx, :] = alpha * acc_s[idx, :] + pv

    for window, dil in DILATED_CFG:
        assert window // dil == blk and (t // dil) % blk == 0
        n_blocks = t // dil // blk
        if n_blocks == 1:
            def res_body(r, carry, dil=dil):
                update(r, dil, False)
                return carry
            lax.fori_loop(0, dil, res_body, 0)
            continue
        for r in range(dil):
            update(r, dil, False)

            def blk_body(i, carry, r=r, dil=dil):
                start = r + i * (blk * dil)
                if dil == 1:
                    start = pl.multiple_of(start, blk)
                update(start, dil, True)
                return carry
            lax.fori_loop(1, n_blocks, blk_body, 0)

    y_ref[...] = (acc_s[...] / l_s[...]).astype(y_ref.dtype)


def _attn_prompt(lidx, z, gq, gk, *, n_heads):
    b, t, n_in = z.shape
    dd = n_heads * HEAD_DIM
    q0 = (n_in - 3 * dd) // HEAD_DIM
    hd = HEAD_DIM
    grid_spec = pltpu.PrefetchScalarGridSpec(
        num_scalar_prefetch=1,
        grid=(b, n_heads),
        in_specs=[pl.BlockSpec((None, t, hd), lambda bi, h, l: (bi, 0, q0 + h)),
                  pl.BlockSpec((None, t, hd), lambda bi, h, l: (bi, 0, q0 + n_heads + h)),
                  pl.BlockSpec((None, t, hd), lambda bi, h, l: (bi, 0, q0 + 2 * n_heads + h)),
                  pl.BlockSpec((None, 1, hd), lambda bi, h, l: (l[0], 0, 0)),
                  pl.BlockSpec((None, 1, hd), lambda bi, h, l: (l[0], 0, 0))],
        out_specs=[pl.BlockSpec((None, t, hd), lambda bi, h, l: (bi, 0, h)),
                   pl.BlockSpec((None, t, hd), lambda bi, h, l: (bi, 0, h))],
        scratch_shapes=[pltpu.VMEM((t, hd), F32)] * 4,
    )
    return pl.pallas_call(
        _attn_prompt_kernel,
        grid_spec=grid_spec,
        out_shape=[jax.ShapeDtypeStruct((b, t, dd), BF16),
                   jax.ShapeDtypeStruct((b, t, dd), F32)],
        compiler_params=_cparams(2),
        name="attn_prompt",
    )(lidx, z, z, z, gq, gk)


def _attn_sample_kernel(l_ref, q_ref, k_ref, v_ref, kc_ref, vc_ref, gq_ref, gk_ref, y_ref, ko_ref, *, t_valid):
    tq = q_ref.shape[0]
    lc = kc_ref.shape[0]
    qn = (_head_rmsnorm(q_ref[...], gq_ref[...]) * (HEAD_DIM ** -0.5)).astype(BF16)
    kn = _head_rmsnorm(k_ref[...], gk_ref[...])
    ko_ref[...] = kn

    def multiplicity(dist):
        w = jnp.zeros(dist.shape, F32)
        for window, dil in DILATED_CFG:
            hit = (dist >= 0) & (dist <= window) & ((dist & (dil - 1)) == 0)
            w = w + hit.astype(F32)
        return w

    s1 = _dot_nt(qn, kc_ref[...].astype(BF16))
    d1 = lc + lax.broadcasted_iota(jnp.int32, (tq, lc), 0) - lax.broadcasted_iota(jnp.int32, (tq, lc), 1)
    w1 = multiplicity(d1)
    s2 = _dot_nt(qn, kn.astype(BF16))
    d2 = lax.broadcasted_iota(jnp.int32, (tq, tq), 0) - lax.broadcasted_iota(jnp.int32, (tq, tq), 1)
    w2 = multiplicity(d2) * (lax.broadcasted_iota(jnp.int32, (tq, tq), 1) < t_valid).astype(F32)
    s1 = jnp.where(w1 > 0, s1, NEG)
    s2 = jnp.where(w2 > 0, s2, NEG)
    m = jnp.maximum(jnp.max(s1, axis=-1, keepdims=True), jnp.max(s2, axis=-1, keepdims=True))
    p1 = w1 * jnp.exp(s1 - m)
    p2 = w2 * jnp.exp(s2 - m)
    den = jnp.sum(p1, axis=-1, keepdims=True) + jnp.sum(p2, axis=-1, keepdims=True)
    num = (jnp.dot(p1.astype(BF16), vc_ref[...].astype(BF16), preferred_element_type=F32)
           + jnp.dot(p2.astype(BF16), v_ref[...].astype(BF16), preferred_element_type=F32))
    y_ref[...] = (num / den).astype(y_ref.dtype)


def _attn_sample(lidx, z, cache_k, cache_v, gq, gk, *, n_heads, t_valid):
    b, tq, n_in = z.shape
    depth, _, lc = cache_k.shape[:3]
    dd = n_heads * HEAD_DIM
    q0 = (n_in - 3 * dd) // HEAD_DIM
    hd = HEAD_DIM
    for window, dil in DILATED_CFG:
        assert dil & (dil - 1) == 0 and window <= lc
    ck = cache_k.reshape(depth, b, lc, dd)
    cv = cache_v.reshape(depth, b, lc, dd)
    grid_spec = pltpu.PrefetchScalarGridSpec(
        num_scalar_prefetch=1,
        grid=(b, n_heads),
        in_specs=[pl.BlockSpec((None, tq, hd), lambda bi, h, l: (bi, 0, q0 + h)),
                  pl.BlockSpec((None, tq, hd), lambda bi, h, l: (bi, 0, q0 + n_heads + h)),
                  pl.BlockSpec((None, tq, hd), lambda bi, h, l: (bi, 0, q0 + 2 * n_heads + h)),
                  pl.BlockSpec((None, None, lc, hd), lambda bi, h, l: (l[0], bi, 0, h)),
                  pl.BlockSpec((None, None, lc, hd), lambda bi, h, l: (l[0], bi, 0, h)),
                  pl.BlockSpec((None, 1, hd), lambda bi, h, l: (l[0], 0, 0)),
                  pl.BlockSpec((None, 1, hd), lambda bi, h, l: (l[0], 0, 0))],
        out_specs=[pl.BlockSpec((None, tq, hd), lambda bi, h, l: (bi, 0, h)),
                   pl.BlockSpec((None, tq, hd), lambda bi, h, l: (bi, 0, h))],
    )
    return pl.pallas_call(
        functools.partial(_attn_sample_kernel, t_valid=t_valid),
        grid_spec=grid_spec,
        out_shape=[jax.ShapeDtypeStruct((b, tq, dd), F32),
                   jax.ShapeDtypeStruct((b, tq, dd), F32)],
        compiler_params=_cparams(2),
        name="attn_sample",
    )(lidx, z, z, z, ck, cv, gq, gk)


SAMPLE_ROWS = 8


def kernel(x_prompt, x_sample, state_pool, state_conv_b, state_conv_c, cache_k, cache_v, g_ffn1, w_ffn1_gate, w_ffn1_up, w_ffn1_down, g_mix, w_in, w_pool, pool_scale, w_dw_b, b_dw_b, ln_b_g, ln_b_b, w_dw_c, q_norm_g, k_norm_g, w_out, g_ffn2, w_ffn2_gate, w_ffn2_up, w_ffn2_down):
    bp, tp, d = x_prompt.shape
    bs, ts, _ = x_sample.shape
    depth = g_ffn1.shape[0]
    dg = pool_scale.shape[-1]
    n_heads = cache_k.shape[3]
    n_in = w_in.shape[-1]
    assert ts <= SAMPLE_ROWS and state_pool.shape[2] == POOL_BUF

    row = lambda a: a.reshape(depth, 1, a.shape[-1])
    g1, gm, g2 = row(g_ffn1), row(g_mix), row(g_ffn2)
    mix_w = (w_pool, row(pool_scale), w_dw_b, row(b_dw_b), row(ln_b_g), row(ln_b_b), w_dw_c)
    gq, gk = row(q_norm_g), row(k_norm_g)

    tm_p, tm_s = 1024, bs * ts
    xp = x_prompt.reshape(bp * tp, d)
    xs = x_sample.reshape(bs * ts, d)
    xgp, ssqp = _prep(xp, g1[0], 256)
    xgs, ssqs = _prep(xs, g1[0], tm_s)

    def ffn(lidx, x, xg, ssq, wg, wu, wd, g_next, tm, tag):
        a = _normed_matmul(lidx, xg, ssq, (wg, wu), tm=tm, tn=256, out_dtype=BF16, name="ffn_up_" + tag)
        return _residual_matmul(lidx, [a], wd, [0], x, g_next, scale=0.5, tm=tm, tn=256, name="ffn_down_" + tag)

    def layer_step(carry, l):
        (xp, xgp, ssqp), (xs, xgs, ssqs) = carry
        lnext = jnp.minimum(l + 1, depth - 1)
        l_same = jnp.stack([l, l]).astype(jnp.int32)
        l_next = jnp.stack([l, lnext]).astype(jnp.int32)

        xp, xgp, ssqp = ffn(l_same, xp, xgp, ssqp, w_ffn1_gate, w_ffn1_up, w_ffn1_down, gm, tm_p, "p")
        zp = _normed_matmul(l_same, xgp, ssqp, (w_in,), tm=tm_p, tn=512, out_dtype=F32, name="proj_in_p")
        zp = zp.reshape(bp, tp, n_in)
        yabc, pool_p, convb_p, convc_p = _mix_abc(
            l_same, zp, None, mix_w, tt=128, tr=32, t_valid=128, n_prev=0, out_dtype=BF16, name="mix_abc_p")
        yd, k_p = _attn_prompt(l_same, zp, gq, gk, n_heads=n_heads)
        v_p = zp[:, :, n_in - dg:]
        xp, xgp, ssqp = _residual_matmul(
            l_same, [yabc.reshape(bp * tp, 3 * dg), yd.reshape(bp * tp, dg)], w_out, [0, 3], xp, g2,
            scale=1.0, tm=tm_p, tn=512, name="proj_out_p")
        xp, xgp, ssqp = ffn(l_next, xp, xgp, ssqp, w_ffn2_gate, w_ffn2_up, w_ffn2_down, g1, tm_p, "p")

        xs, xgs, ssqs = ffn(l_same, xs, xgs, ssqs, w_ffn1_gate, w_ffn1_up, w_ffn1_down, gm, tm_s, "s")
        zs = _normed_matmul(l_same, xgs, ssqs, (w_in,), tm=tm_s, tn=512, out_dtype=F32, name="proj_in_s")
        zs = jnp.pad(zs.reshape(bs, ts, n_in), ((0, 0), (0, SAMPLE_ROWS - ts), (0, 0)))
        yabc_s, pool_s, convb_s, convc_s = _mix_abc(
            l_same, zs, (state_pool, state_conv_b, state_conv_c), mix_w, tt=SAMPLE_ROWS, tr=SAMPLE_ROWS,
            t_valid=ts, n_prev=POOL_BUF, out_dtype=F32, name="mix_abc_s")
        yd_s, k_s = _attn_sample(l_same, zs, cache_k, cache_v, gq, gk, n_heads=n_heads, t_valid=ts)
        v_s = zs[:, :ts, n_in - dg:]
        a_s = [yabc_s[:, :ts].reshape(bs * ts, 3 * dg).astype(BF16), yd_s[:, :ts].reshape(bs * ts, dg).astype(BF16)]
        xs, xgs, ssqs = _residual_matmul(l_same, a_s, w_out, [0, 3], xs, g2, scale=1.0, tm=tm_s, tn=512,
                                         name="proj_out_s")
        xs, xgs, ssqs = ffn(l_next, xs, xgs, ssqs, w_ffn2_gate, w_ffn2_up, w_ffn2_down, g1, tm_s, "s")

        outs = (pool_p, pool_s, convb_p, convb_s, convc_p, convc_s,
                k_p.reshape(bp, tp, n_heads, HEAD_DIM), v_p.reshape(bp, tp, n_heads, HEAD_DIM),
                k_s[:, :ts].reshape(bs, ts, n_heads, HEAD_DIM), v_s.reshape(bs, ts, n_heads, HEAD_DIM))
        return ((xp, xgp, ssqp), (xs, xgs, ssqs)), outs

    carry, outs = lax.scan(layer_step, ((xp, xgp, ssqp), (xs, xgs, ssqs)), jnp.arange(depth, dtype=jnp.int32))
    (xp, _, _), (xs, _, _) = carry
    return (xp.reshape(bp, tp, d), xs.reshape(bs, ts, d)) + tuple(outs)
```

```python
import functools

import jax
import jax.numpy as jnp
from jax import lax
from jax.experimental import pallas as pl
from jax.experimental.pallas import tpu as pltpu

EPS = 1e-6
NEG = -1e30
LANES = 128
SUBLANES = 8
HEAD_DIM = 128
POOL_WINDOWS = (2, 4, 8, 16)
POOL_BUF = max(POOL_WINDOWS) - 1
CONV_B_WIDTH = 31
CONV_C_WIDTH = 3
DILATED_CFG = ((128, 1), (512, 4), (2048, 16))
ATT_BLK = 128
VMEM_LIMIT = 60 * 1024 * 1024

F32 = jnp.float32
BF16 = jnp.bfloat16


def _cparams(n_axes):
    return pltpu.CompilerParams(dimension_semantics=("arbitrary",) * n_axes,
                                vmem_limit_bytes=VMEM_LIMIT)


def _lane_fold(v):
    n = v.shape[-1] // LANES
    out = v[:, 0:LANES]
    for k in range(1, n):
        out = out + v[:, k * LANES:(k + 1) * LANES]
    return out


def _row_rms_scale(ssq, d_model):
    return lax.rsqrt(jnp.sum(ssq, axis=-1, keepdims=True) * (1.0 / d_model) + EPS)


def _prep_kernel(x_ref, g_ref, xg_ref, ssq_ref):
    x = x_ref[...]
    xg_ref[...] = (x * g_ref[...]).astype(BF16)
    ssq_ref[...] = _lane_fold(x * x)


def _prep(x, g_row, tm):
    m, d = x.shape
    return pl.pallas_call(
        _prep_kernel,
        grid=(m // tm,),
        in_specs=[pl.BlockSpec((tm, d), lambda i: (i, 0)),
                  pl.BlockSpec((1, d), lambda i: (0, 0))],
        out_specs=[pl.BlockSpec((tm, d), lambda i: (i, 0)),
                   pl.BlockSpec((tm, LANES), lambda i: (i, 0))],
        out_shape=[jax.ShapeDtypeStruct((m, d), BF16),
                   jax.ShapeDtypeStruct((m, LANES), F32)],
        compiler_params=_cparams(1),
        name="norm_prep",
    )(x, g_row)


def _side_col(nj):
    return lambda i, j: jnp.where(i == 0, j, nj - 1)


def _up_kernel(l_ref, xg_ref, ssq_ref, xgs_ref, ssqs_ref, *rest, n_w, d_model):
    w_refs, o_ref, os_ref = rest[:n_w], rest[n_w], rest[n_w + 1]
    wb = [w[...].astype(BF16) for w in w_refs]

    def compute(xg, ssq):
        r = _row_rms_scale(ssq, d_model)
        outs = [jnp.dot(xg, w, preferred_element_type=F32) * r for w in wb]
        return outs[0] if n_w == 1 else jax.nn.silu(outs[0]) * outs[1]

    o_ref[...] = compute(xg_ref[...], ssq_ref[...]).astype(o_ref.dtype)

    @pl.when(pl.program_id(0) == 0)
    def _():
        os_ref[...] = compute(xgs_ref[...], ssqs_ref[...]).astype(os_ref.dtype)


def _normed_matmul(lidx, xg, ssq, xgs, ssqs, ws, *, tm, tn, out_dtype, name):
    m, k = xg.shape
    ms = xgs.shape[0]
    n = ws[0].shape[-1]
    sj = _side_col(n // tn)
    grid_spec = pltpu.PrefetchScalarGridSpec(
        num_scalar_prefetch=1,
        grid=(m // tm, n // tn),
        in_specs=[pl.BlockSpec((tm, k), lambda i, j, l: (i, 0)),
                  pl.BlockSpec((tm, LANES), lambda i, j, l: (i, 0)),
                  pl.BlockSpec((ms, k), lambda i, j, l: (0, 0)),
                  pl.BlockSpec((ms, LANES), lambda i, j, l: (0, 0))]
                 + [pl.BlockSpec((None, k, tn), lambda i, j, l: (l[0], 0, j)) for _ in ws],
        out_specs=[pl.BlockSpec((tm, tn), lambda i, j, l: (i, j)),
                   pl.BlockSpec((ms, tn), lambda i, j, l: (0, sj(i, j)))],
    )
    return pl.pallas_call(
        functools.partial(_up_kernel, n_w=len(ws), d_model=k),
        grid_spec=grid_spec,
        out_shape=[jax.ShapeDtypeStruct((m, n), out_dtype),
                   jax.ShapeDtypeStruct((ms, n), out_dtype)],
        compiler_params=_cparams(2),
        name=name,
    )(lidx, xg, ssq, xgs, ssqs, *ws)


def _down_kernel(l_ref, *refs, n_a, scale):
    a_refs, as_refs, w_refs = refs[:n_a], refs[n_a:2 * n_a], refs[2 * n_a:3 * n_a]
    res_ref, ress_ref, g_ref, x_out, xg_out, ssq_out, xs_out, xgs_out, ssqs_out = refs[3 * n_a:]
    wb = [w[...].astype(BF16) for w in w_refs]
    first_col = pl.program_id(1) == 0

    def compute(a_group, res, x_o, xg_o, ssq_o):
        acc = None
        for a, w in zip(a_group, wb):
            part = jnp.dot(a[...], w, preferred_element_type=F32)
            acc = part if acc is None else acc + part
        xn = res[...] + scale * acc
        x_o[...] = xn
        xg_o[...] = (xn * g_ref[...]).astype(BF16)
        part_ssq = _lane_fold(xn * xn)

        @pl.when(first_col)
        def _():
            ssq_o[...] = part_ssq

        @pl.when(jnp.logical_not(first_col))
        def _():
            ssq_o[...] += part_ssq

    compute(a_refs, res_ref, x_out, xg_out, ssq_out)

    @pl.when(pl.program_id(0) == 0)
    def _():
        compute(as_refs, ress_ref, xs_out, xgs_out, ssqs_out)


def _residual_matmul(lidx, a_list, as_list, w, row_blocks, res, ress, g_next, *, scale, tm, tn, name):
    m, ms = res.shape[0], ress.shape[0]
    n = w.shape[-1]
    sj = _side_col(n // tn)
    in_specs = [pl.BlockSpec((tm, a.shape[1]), lambda i, j, l: (i, 0), pipeline_mode=pl.Buffered(1))
                for a in a_list]
    in_specs += [pl.BlockSpec((ms, a.shape[1]), lambda i, j, l: (0, 0)) for a in as_list]
    in_specs += [pl.BlockSpec((None, a.shape[1], tn), functools.partial(lambda i, j, l, rb: (l[0], rb, j), rb=rb))
                 for a, rb in zip(a_list, row_blocks)]
    in_specs += [pl.BlockSpec((tm, tn), lambda i, j, l: (i, j)),
                 pl.BlockSpec((ms, tn), lambda i, j, l: (0, sj(i, j))),
                 pl.BlockSpec((None, 1, tn), lambda i, j, l: (l[1], 0, j))]
    grid_spec = pltpu.PrefetchScalarGridSpec(
        num_scalar_prefetch=1,
        grid=(m // tm, n // tn),
        in_specs=in_specs,
        out_specs=[pl.BlockSpec((tm, tn), lambda i, j, l: (i, j)),
                   pl.BlockSpec((tm, tn), lambda i, j, l: (i, j)),
                   pl.BlockSpec((tm, LANES), lambda i, j, l: (i, 0)),
                   pl.BlockSpec((ms, tn), lambda i, j, l: (0, sj(i, j))),
                   pl.BlockSpec((ms, tn), lambda i, j, l: (0, sj(i, j))),
                   pl.BlockSpec((ms, LANES), lambda i, j, l: (0, 0))],
    )
    return pl.pallas_call(
        functools.partial(_down_kernel, n_a=len(a_list), scale=scale),
        grid_spec=grid_spec,
        out_shape=[jax.ShapeDtypeStruct((m, n), F32),
                   jax.ShapeDtypeStruct((m, n), BF16),
                   jax.ShapeDtypeStruct((m, LANES), F32),
                   jax.ShapeDtypeStruct((ms, n), F32),
                   jax.ShapeDtypeStruct((ms, n), BF16),
                   jax.ShapeDtypeStruct((ms, LANES), F32)],
        compiler_params=_cparams(2),
        name=name,
    )(lidx, *a_list, *as_list, *([w] * len(a_list)), res, ress, g_next)


HIST_A = 16
HIST_B = 32
HIST_C = 8


def _mix_abc_kernel(l_ref, *refs, tt, tr, t_valid, n_prev, n_chunks, has_state, dg):
    if has_state:
        (z_ref, sp_ref, sb_ref, sc_ref, wpool_ref, pscale_ref, wdwb_ref, bdwb_ref, lng_ref, lnb_ref, wdwc_ref,
         y_ref, np_ref, nb_ref, nc_ref, full_a, full_b, full_c, shift_b, pooled_s) = refs
    else:
        (z_ref, wpool_ref, pscale_ref, wdwb_ref, bdwb_ref, lng_ref, lnb_ref, wdwc_ref,
         y_ref, np_ref, nb_ref, nc_ref, full_a, full_b, full_c, shift_b, pooled_s) = refs
    c = pl.program_id(1)

    @pl.when(c == 0)
    def _():
        full_a[0:HIST_A, :] = jnp.zeros((HIST_A, dg), F32)
        full_b[0:HIST_B, :] = jnp.zeros((HIST_B, dg), F32)
        full_c[0:HIST_C, :] = jnp.zeros((HIST_C, dg), F32)
        if has_state:
            full_a[HIST_A - POOL_BUF:HIST_A, :] = sp_ref[...]
            full_b[HIST_B - (CONV_B_WIDTH - 1):HIST_B, :] = sb_ref[...]
            full_c[HIST_C - (CONV_C_WIDTH - 1):HIST_C, :] = sc_ref[...]

    full_a[HIST_A:HIST_A + tt, :] = z_ref[:, 0:dg]
    full_b[HIST_B:HIST_B + tt, :] = z_ref[:, dg:2 * dg] * jax.nn.sigmoid(z_ref[:, 2 * dg:3 * dg])
    full_c[HIST_C:HIST_C + tt, :] = z_ref[:, 4 * dg:5 * dg] * z_ref[:, 5 * dg:6 * dg]

    n_shift = HIST_B + tt - SUBLANES
    for s in range(1, SUBLANES):
        shift_b[s - 1, :, :] = full_b[s:s + n_shift, :]

    pch = dg // len(POOL_WINDOWS)
    for r0 in range(0, tt, tr):
        t_glob = c * tt + r0 + lax.broadcasted_iota(jnp.int32, (tr, 1), 0)
        for g, w in enumerate(POOL_WINDOWS):
            cols = slice(g * pch, (g + 1) * pch)
            cur = full_a[HIST_A + r0:HIST_A + r0 + tr, cols]
            acc = cur
            for s in range(1, w):
                acc = acc + full_a[HIST_A + r0 - s:HIST_A + r0 - s + tr, cols]
            cnt = jnp.minimum(w, n_prev + t_glob + 1).astype(F32)
            pooled_s[r0:r0 + tr, cols] = acc / cnt - cur

        acc = jnp.zeros((tr, dg), F32) + bdwb_ref[...]
        for j in range(CONV_B_WIDTH):
            off = HIST_B - (CONV_B_WIDTH - 1) + j
            base, s = r0 + off - off % SUBLANES, off % SUBLANES
            src = full_b[base:base + tr, :] if s == 0 else shift_b[s - 1, base:base + tr, :]
            acc = acc + wdwb_ref[j:j + 1, :] * src
        mu = jnp.mean(acc, axis=-1, keepdims=True)
        xc = acc - mu
        var = jnp.mean(xc * xc, axis=-1, keepdims=True)
        yn = xc * lax.rsqrt(var + EPS) * lng_ref[...] + lnb_ref[...]
        y_ref[r0:r0 + tr, dg:2 * dg] = jax.nn.silu(yn).astype(y_ref.dtype)

        acc = jnp.zeros((tr, dg), F32)
        off = HIST_C - (CONV_C_WIDTH - 1) + r0
        for j in range(CONV_C_WIDTH):
            acc = acc + wdwc_ref[j:j + 1, :] * full_c[off + j:off + j + tr, :]
        y_ref[r0:r0 + tr, 2 * dg:3 * dg] = (z_ref[r0:r0 + tr, 3 * dg:4 * dg] * acc).astype(y_ref.dtype)

    for g in range(len(POOL_WINDOWS)):
        cols = slice(g * pch, (g + 1) * pch)
        ya = jnp.dot(pooled_s[:, cols].astype(BF16), wpool_ref[g].astype(BF16), preferred_element_type=F32)
        y_ref[:, cols] = (ya * pscale_ref[:, cols]).astype(y_ref.dtype)

    @pl.when(c == n_chunks - 1)
    def _():
        np_ref[...] = full_a[HIST_A + t_valid - POOL_BUF:HIST_A + t_valid, :]
        nb_ref[...] = full_b[HIST_B + t_valid - (CONV_B_WIDTH - 1):HIST_B + t_valid, :]
        nc_ref[...] = full_c[HIST_C + t_valid - (CONV_C_WIDTH - 1):HIST_C + t_valid, :]

    if n_chunks > 1:
        @pl.when(c < n_chunks - 1)
        def _():
            full_a[0:HIST_A, :] = full_a[tt:tt + HIST_A, :]
            full_b[0:HIST_B, :] = full_b[tt:tt + HIST_B, :]
            full_c[0:HIST_C, :] = full_c[tt:tt + HIST_C, :]


def _mix_abc(lidx, z, states, wts, *, tt, tr, t_valid, n_prev, out_dtype, name):
    b, t, _ = z.shape
    w_pool, pool_scale, w_dw_b, b_dw_b, ln_g, ln_b, w_dw_c = wts
    dg = pool_scale.shape[-1]
    n_chunks = t // tt
    has_state = states is not None

    def lsel(*tail):
        return lambda bi, ci, l: (l[0],) + tail

    in_specs = [pl.BlockSpec((None, tt, 6 * dg), lambda bi, ci, l: (bi, ci, 0))]
    args = [z]
    if has_state:
        for s in states:
            in_specs.append(pl.BlockSpec((None, None) + s.shape[2:], lambda bi, ci, l: (l[0], bi, 0, 0)))
            args.append(s)
    in_specs += [
        pl.BlockSpec((None,) + w_pool.shape[1:], lsel(0, 0, 0)),
        pl.BlockSpec((None, 1, dg), lsel(0, 0)),
        pl.BlockSpec((None, CONV_B_WIDTH, dg), lsel(0, 0)),
        pl.BlockSpec((None, 1, dg), lsel(0, 0)),
        pl.BlockSpec((None, 1, dg), lsel(0, 0)),
        pl.BlockSpec((None, 1, dg), lsel(0, 0)),
        pl.BlockSpec((None, CONV_C_WIDTH, dg), lsel(0, 0)),
    ]
    args += [w_pool, pool_scale, w_dw_b, b_dw_b, ln_g, ln_b, w_dw_c]
    grid_spec = pltpu.PrefetchScalarGridSpec(
        num_scalar_prefetch=1,
        grid=(b, n_chunks),
        in_specs=in_specs,
        out_specs=[pl.BlockSpec((None, tt, 3 * dg), lambda bi, ci, l: (bi, ci, 0)),
                   pl.BlockSpec((None, POOL_BUF, dg), lambda bi, ci, l: (bi, 0, 0)),
                   pl.BlockSpec((None, CONV_B_WIDTH - 1, dg), lambda bi, ci, l: (bi, 0, 0)),
                   pl.BlockSpec((None, CONV_C_WIDTH - 1, dg), lambda bi, ci, l: (bi, 0, 0))],
        scratch_shapes=[pltpu.VMEM((HIST_A + tt, dg), F32),
                        pltpu.VMEM((HIST_B + tt, dg), F32),
                        pltpu.VMEM((HIST_C + tt, dg), F32),
                        pltpu.VMEM((SUBLANES - 1, HIST_B + tt - SUBLANES, dg), F32),
                        pltpu.VMEM((tt, dg), F32)],
    )
    return pl.pallas_call(
        functools.partial(_mix_abc_kernel, tt=tt, tr=tr, t_valid=t_valid, n_prev=n_prev,
                          n_chunks=n_chunks, has_state=has_state, dg=dg),
        grid_spec=grid_spec,
        out_shape=[jax.ShapeDtypeStruct((b, t, 3 * dg), out_dtype),
                   jax.ShapeDtypeStruct((b, POOL_BUF, dg), F32),
                   jax.ShapeDtypeStruct((b, CONV_B_WIDTH - 1, dg), F32),
                   jax.ShapeDtypeStruct((b, CONV_C_WIDTH - 1, dg), F32)],
        compiler_params=_cparams(2),
        name=name,
    )(lidx, *args)


def _head_rmsnorm(x, g):
    return x * lax.rsqrt(jnp.mean(x * x, axis=-1, keepdims=True) + EPS) * g


def _dot_nt(a, b):
    return lax.dot_general(a, b, (((1,), (1,)), ((), ())), preferred_element_type=F32)


def _attn_prompt_kernel(l_ref, q_ref, k_ref, v_ref, gq_ref, gk_ref, y_ref, ko_ref, qn_s, m_s, l_s, acc_s):
    t = q_ref.shape[0]
    blk = ATT_BLK
    qn_s[...] = _head_rmsnorm(q_ref[...], gq_ref[...]) * (HEAD_DIM ** -0.5)
    ko_ref[...] = _head_rmsnorm(k_ref[...], gk_ref[...])

    row1 = lax.broadcasted_iota(jnp.int32, (blk, blk), 0)
    col1 = lax.broadcasted_iota(jnp.int32, (blk, blk), 1)
    causal_mask = col1 <= row1
    row2 = lax.broadcasted_iota(jnp.int32, (blk, 2 * blk), 0)
    col2 = lax.broadcasted_iota(jnp.int32, (blk, 2 * blk), 1)
    band_mask = (col2 >= row2) & (col2 <= row2 + blk)

    def rows(start, n, dil):
        return pl.ds(start, n, stride=dil) if dil > 1 else pl.ds(start, n)

    def update(start, dil, with_prev, first_branch):
        qidx = rows(start, blk, dil)
        kidx = rows(start - blk * dil, 2 * blk, dil) if with_prev else qidx
        qb = qn_s[qidx, :].astype(BF16)
        s = _dot_nt(qb, ko_ref[kidx, :].astype(BF16))
        s = jnp.where(band_mask if with_prev else causal_mask, s, NEG)
        mx = jnp.max(s, axis=-1, keepdims=True)
        vb = v_ref[kidx, :].astype(BF16)
        if first_branch:
            p = jnp.exp(s - mx)
            m_s[qidx, :] = jnp.broadcast_to(mx, (blk, LANES))
            l_s[qidx, :] = jnp.broadcast_to(jnp.sum(p, axis=-1, keepdims=True), (blk, LANES))
            acc_s[qidx, :] = jnp.dot(p.astype(BF16), vb, preferred_element_type=F32)
        else:
            m_old = m_s[qidx, :]
            m_new = jnp.maximum(m_old, mx)
            alpha = jnp.exp(m_old - m_new)
            p = jnp.exp(s - (jnp.concatenate([m_new, m_new], axis=-1) if with_prev else m_new))
            m_s[qidx, :] = m_new
            l_s[qidx, :] = alpha * l_s[qidx, :] + jnp.sum(p, axis=-1, keepdims=True)
            acc_s[qidx, :] = alpha * acc_s[qidx, :] + jnp.dot(p.astype(BF16), vb, preferred_element_type=F32)

    for bi, (window, dil) in enumerate(DILATED_CFG):
        assert window // dil == blk and (t // dil) % blk == 0
        for r in range(dil):
            for i in range(t // dil // blk):
                update(r + i * blk * dil, dil, i > 0, bi == 0)

    y_ref[...] = (acc_s[...] / l_s[...]).astype(y_ref.dtype)


def _attn_prompt(lidx, z, gq, gk, *, n_heads):
    b, t, n_in = z.shape
    dd = n_heads * HEAD_DIM
    q0 = (n_in - 3 * dd) // HEAD_DIM
    hd = HEAD_DIM
    grid_spec = pltpu.PrefetchScalarGridSpec(
        num_scalar_prefetch=1,
        grid=(b, n_heads),
        in_specs=[pl.BlockSpec((None, t, hd), lambda bi, h, l: (bi, 0, q0 + h)),
                  pl.BlockSpec((None, t, hd), lambda bi, h, l: (bi, 0, q0 + n_heads + h)),
                  pl.BlockSpec((None, t, hd), lambda bi, h, l: (bi, 0, q0 + 2 * n_heads + h)),
                  pl.BlockSpec((None, 1, hd), lambda bi, h, l: (l[0], 0, 0)),
                  pl.BlockSpec((None, 1, hd), lambda bi, h, l: (l[0], 0, 0))],
        out_specs=[pl.BlockSpec((None, t, hd), lambda bi, h, l: (bi, 0, h)),
                   pl.BlockSpec((None, t, hd), lambda bi, h, l: (bi, 0, h))],
        scratch_shapes=[pltpu.VMEM((t, hd), F32)] * 4,
    )
    return pl.pallas_call(
        _attn_prompt_kernel,
        grid_spec=grid_spec,
        out_shape=[jax.ShapeDtypeStruct((b, t, dd), BF16),
                   jax.ShapeDtypeStruct((b, t, dd), F32)],
        compiler_params=_cparams(2),
        name="attn_prompt",
    )(lidx, z, z, z, gq, gk)


def _attn_sample_kernel(l_ref, qkv_ref, kc_ref, vc_ref, gq_ref, gk_ref, y_ref, ko_ref, *, t_valid, n_heads):
    tq = qkv_ref.shape[0]
    hd = HEAD_DIM
    lc = kc_ref.shape[0] // n_heads

    def multiplicity(dist):
        w = jnp.zeros(dist.shape, F32)
        for window, dil in DILATED_CFG:
            hit = (dist >= 0) & (dist <= window) & ((dist & (dil - 1)) == 0)
            w = w + hit.astype(F32)
        return w

    d1 = lc + lax.broadcasted_iota(jnp.int32, (tq, lc), 0) - lax.broadcasted_iota(jnp.int32, (tq, lc), 1)
    w1 = multiplicity(d1)
    d2 = lax.broadcasted_iota(jnp.int32, (tq, tq), 0) - lax.broadcasted_iota(jnp.int32, (tq, tq), 1)
    w2 = multiplicity(d2) * (lax.broadcasted_iota(jnp.int32, (tq, tq), 1) < t_valid).astype(F32)

    for h in range(n_heads):
        q = qkv_ref[:, h * hd:(h + 1) * hd]
        k = qkv_ref[:, (n_heads + h) * hd:(n_heads + h + 1) * hd]
        v = qkv_ref[:, (2 * n_heads + h) * hd:(2 * n_heads + h + 1) * hd]
        qn = (_head_rmsnorm(q, gq_ref[...]) * (HEAD_DIM ** -0.5)).astype(BF16)
        kn = _head_rmsnorm(k, gk_ref[...])
        ko_ref[:, h * hd:(h + 1) * hd] = kn
        head_rows = pl.ds(h, lc, stride=n_heads)
        s1 = jnp.where(w1 > 0, _dot_nt(qn, kc_ref[head_rows, :].astype(BF16)), NEG)
        s2 = jnp.where(w2 > 0, _dot_nt(qn, kn.astype(BF16)), NEG)
        m = jnp.maximum(jnp.max(s1, axis=-1, keepdims=True), jnp.max(s2, axis=-1, keepdims=True))
        p1 = w1 * jnp.exp(s1 - m)
        p2 = w2 * jnp.exp(s2 - m)
        den = jnp.sum(p1, axis=-1, keepdims=True) + jnp.sum(p2, axis=-1, keepdims=True)
        num = (jnp.dot(p1.astype(BF16), vc_ref[head_rows, :].astype(BF16), preferred_element_type=F32)
               + jnp.dot(p2.astype(BF16), v.astype(BF16), preferred_element_type=F32))
        y_ref[:, h * hd:(h + 1) * hd] = (num / den).astype(y_ref.dtype)


def _attn_sample(lidx, z, cache_k, cache_v, gq, gk, *, n_heads, t_valid):
    b, tq, n_in = z.shape
    depth, _, lc = cache_k.shape[:3]
    hd = HEAD_DIM
    dd = n_heads * hd
    assert (n_in - 3 * dd) % (3 * dd) == 0
    qkv_block = (n_in - 3 * dd) // (3 * dd)
    for window, dil in DILATED_CFG:
        assert dil & (dil - 1) == 0 and window <= lc
    ck = cache_k.reshape(depth, b, lc * n_heads, hd)
    cv = cache_v.reshape(depth, b, lc * n_heads, hd)
    grid_spec = pltpu.PrefetchScalarGridSpec(
        num_scalar_prefetch=1,
        grid=(b,),
        in_specs=[pl.BlockSpec((None, tq, 3 * dd), lambda bi, l: (bi, 0, qkv_block)),
                  pl.BlockSpec((None, None, lc * n_heads, hd), lambda bi, l: (l[0], bi, 0, 0)),
                  pl.BlockSpec((None, None, lc * n_heads, hd), lambda bi, l: (l[0], bi, 0, 0)),
                  pl.BlockSpec((None, 1, hd), lambda bi, l: (l[0], 0, 0)),
                  pl.BlockSpec((None, 1, hd), lambda bi, l: (l[0], 0, 0))],
        out_specs=[pl.BlockSpec((None, tq, dd), lambda bi, l: (bi, 0, 0)),
                   pl.BlockSpec((None, tq, dd), lambda bi, l: (bi, 0, 0))],
    )
    return pl.pallas_call(
        functools.partial(_attn_sample_kernel, t_valid=t_valid, n_heads=n_heads),
        grid_spec=grid_spec,
        out_shape=[jax.ShapeDtypeStruct((b, tq, dd), F32),
                   jax.ShapeDtypeStruct((b, tq, dd), F32)],
        compiler_params=_cparams(1),
        name="attn_sample",
    )(lidx, z, ck, cv, gq, gk)


SAMPLE_ROWS = 8
TM = 1024
TN_UP = 256
TN_WIDE = 512


def kernel(x_prompt, x_sample, state_pool, state_conv_b, state_conv_c, cache_k, cache_v, g_ffn1, w_ffn1_gate, w_ffn1_up, w_ffn1_down, g_mix, w_in, w_pool, pool_scale, w_dw_b, b_dw_b, ln_b_g, ln_b_b, w_dw_c, q_norm_g, k_norm_g, w_out, g_ffn2, w_ffn2_gate, w_ffn2_up, w_ffn2_down):
    bp, tp, d = x_prompt.shape
    bs, ts, _ = x_sample.shape
    depth = g_ffn1.shape[0]
    dg = pool_scale.shape[-1]
    n_heads = cache_k.shape[3]
    n_in = w_in.shape[-1]
    assert ts <= SAMPLE_ROWS and state_pool.shape[2] == POOL_BUF

    row = lambda a: a.reshape(depth, 1, a.shape[-1])
    g1, gm, g2 = row(g_ffn1), row(g_mix), row(g_ffn2)
    mix_w = (w_pool, row(pool_scale), w_dw_b, row(b_dw_b), row(ln_b_g), row(ln_b_b), w_dw_c)
    gq, gk = row(q_norm_g), row(k_norm_g)

    xp = x_prompt.reshape(bp * tp, d)
    xs = x_sample.reshape(bs * ts, d)
    xgp, ssqp = _prep(xp, g1[0], 256)
    xgs, ssqs = _prep(xs, g1[0], bs * ts)

    def ffn(lidx, st, wg, wu, wd, g_next, tag):
        xp, xgp, ssqp, xs, xgs, ssqs = st
        ap, a_s = _normed_matmul(lidx, xgp, ssqp, xgs, ssqs, (wg, wu), tm=TM, tn=TN_UP, out_dtype=BF16,
                                 name="ffn_up" + tag)
        return tuple(_residual_matmul(lidx, [ap], [a_s], wd, [0], xp, xs, g_next, scale=0.5, tm=TM, tn=TN_UP,
                                      name="ffn_down" + tag))

    def layer_step(st, l):
        lnext = jnp.minimum(l + 1, depth - 1)
        l_same = jnp.stack([l, l]).astype(jnp.int32)
        l_next = jnp.stack([l, lnext]).astype(jnp.int32)

        st = ffn(l_same, st, w_ffn1_gate, w_ffn1_up, w_ffn1_down, gm, "1")
        xp, xgp, ssqp, xs, xgs, ssqs = st
        zp, zs = _normed_matmul(l_same, xgp, ssqp, xgs, ssqs, (w_in,), tm=TM, tn=TN_WIDE, out_dtype=F32,
                                name="proj_in")

        zp = zp.reshape(bp, tp, n_in)
        yabc, pool_p, convb_p, convc_p = _mix_abc(
            l_same, zp, None, mix_w, tt=128, tr=16, t_valid=128, n_prev=0, out_dtype=BF16, name="mix_abc_p")
        yd, k_p = _attn_prompt(l_same, zp, gq, gk, n_heads=n_heads)
        v_p = zp[:, :, n_in - dg:]

        zs = jnp.pad(zs.reshape(bs, ts, n_in), ((0, 0), (0, SAMPLE_ROWS - ts), (0, 0)))
        yabc_s, pool_s, convb_s, convc_s = _mix_abc(
            l_same, zs, (state_pool, state_conv_b, state_conv_c), mix_w, tt=SAMPLE_ROWS, tr=SAMPLE_ROWS,
            t_valid=ts, n_prev=POOL_BUF, out_dtype=F32, name="mix_abc_s")
        yd_s, k_s = _attn_sample(l_same, zs, cache_k, cache_v, gq, gk, n_heads=n_heads, t_valid=ts)
        v_s = zs[:, :ts, n_in - dg:]

        a_p = [yabc.reshape(bp * tp, 3 * dg), yd.reshape(bp * tp, dg)]
        a_s = [yabc_s[:, :ts].reshape(bs * ts, 3 * dg).astype(BF16), yd_s[:, :ts].reshape(bs * ts, dg).astype(BF16)]
        st = tuple(_residual_matmul(l_same, a_p, a_s, w_out, [0, 3], xp, xs, g2, scale=1.0, tm=TM, tn=TN_WIDE,
                                    name="proj_out"))
        st = ffn(l_next, st, w_ffn2_gate, w_ffn2_up, w_ffn2_down, g1, "2")

        outs = (pool_p, pool_s, convb_p, convb_s, convc_p, convc_s,
                k_p.reshape(bp, tp, n_heads, HEAD_DIM), v_p.reshape(bp, tp, n_heads, HEAD_DIM),
                k_s[:, :ts].reshape(bs, ts, n_heads, HEAD_DIM), v_s.reshape(bs, ts, n_heads, HEAD_DIM))
        return st, outs

    st, outs = lax.scan(layer_step, (xp, xgp, ssqp, xs, xgs, ssqs), jnp.arange(depth, dtype=jnp.int32))
    return (st[0].reshape(bp, tp, d), st[3].reshape(bs, ts, d)) + tuple(outs)
```

```python
import functools

import jax
import jax.numpy as jnp
from jax import lax
from jax.experimental import pallas as pl
from jax.experimental.pallas import tpu as pltpu

EPS = 1e-6
NEG = -1e30
LANES = 128
SUBLANES = 8
HEAD_DIM = 128
POOL_WINDOWS = (2, 4, 8, 16)
POOL_BUF = max(POOL_WINDOWS) - 1
CONV_B_WIDTH = 31
CONV_C_WIDTH = 3
DILATED_CFG = ((128, 1), (512, 4), (2048, 16))
ATT_BLK = 128
VMEM_LIMIT = 60 * 1024 * 1024

F32 = jnp.float32
BF16 = jnp.bfloat16


def _cparams(n_axes):
    return pltpu.CompilerParams(dimension_semantics=("arbitrary",) * n_axes,
                                vmem_limit_bytes=VMEM_LIMIT)


def _lane_fold(v):
    n = v.shape[-1] // LANES
    out = v[:, 0:LANES]
    for k in range(1, n):
        out = out + v[:, k * LANES:(k + 1) * LANES]
    return out


def _row_rms_scale(ssq, d_model):
    return lax.rsqrt(jnp.sum(ssq, axis=-1, keepdims=True) * (1.0 / d_model) + EPS)


def _prep_kernel(x_ref, g_ref, xg_ref, ssq_ref):
    x = x_ref[...]
    xg_ref[...] = (x * g_ref[...]).astype(BF16)
    ssq_ref[...] = _lane_fold(x * x)


def _prep(x, g_row, tm):
    m, d = x.shape
    return pl.pallas_call(
        _prep_kernel,
        grid=(m // tm,),
        in_specs=[pl.BlockSpec((tm, d), lambda i: (i, 0)),
                  pl.BlockSpec((1, d), lambda i: (0, 0))],
        out_specs=[pl.BlockSpec((tm, d), lambda i: (i, 0)),
                   pl.BlockSpec((tm, LANES), lambda i: (i, 0))],
        out_shape=[jax.ShapeDtypeStruct((m, d), BF16),
                   jax.ShapeDtypeStruct((m, LANES), F32)],
        compiler_params=_cparams(1),
        name="norm_prep",
    )(x, g_row)


def _side_col(nj):
    return lambda i, j: jnp.where(i == 0, j, nj - 1)


def _up_kernel(l_ref, xg_ref, ssq_ref, xgs_ref, ssqs_ref, *rest, n_w, d_model):
    w_refs, o_ref, os_ref = rest[:n_w], rest[n_w], rest[n_w + 1]
    wb = [w[...].astype(BF16) for w in w_refs]

    def compute(xg, ssq):
        r = _row_rms_scale(ssq, d_model)
        outs = [jnp.dot(xg, w, preferred_element_type=F32) * r for w in wb]
        return outs[0] if n_w == 1 else jax.nn.silu(outs[0]) * outs[1]

    o_ref[...] = compute(xg_ref[...], ssq_ref[...]).astype(o_ref.dtype)

    @pl.when(pl.program_id(0) == 0)
    def _():
        os_ref[...] = compute(xgs_ref[...], ssqs_ref[...]).astype(os_ref.dtype)


def _normed_matmul(lidx, xg, ssq, xgs, ssqs, ws, *, tm, tn, out_dtype, name):
    m, k = xg.shape
    ms = xgs.shape[0]
    n = ws[0].shape[-1]
    sj = _side_col(n // tn)
    grid_spec = pltpu.PrefetchScalarGridSpec(
        num_scalar_prefetch=1,
        grid=(m // tm, n // tn),
        in_specs=[pl.BlockSpec((tm, k), lambda i, j, l: (i, 0), pipeline_mode=pl.Buffered(1)),
                  pl.BlockSpec((tm, LANES), lambda i, j, l: (i, 0)),
                  pl.BlockSpec((ms, k), lambda i, j, l: (0, 0)),
                  pl.BlockSpec((ms, LANES), lambda i, j, l: (0, 0))]
                 + [pl.BlockSpec((None, k, tn), lambda i, j, l: (l[0], 0, j)) for _ in ws],
        out_specs=[pl.BlockSpec((tm, tn), lambda i, j, l: (i, j)),
                   pl.BlockSpec((ms, tn), lambda i, j, l: (0, sj(i, j)))],
    )
    return pl.pallas_call(
        functools.partial(_up_kernel, n_w=len(ws), d_model=k),
        grid_spec=grid_spec,
        out_shape=[jax.ShapeDtypeStruct((m, n), out_dtype),
                   jax.ShapeDtypeStruct((ms, n), out_dtype)],
        compiler_params=_cparams(2),
        name=name,
    )(lidx, xg, ssq, xgs, ssqs, *ws)


def _down_kernel(l_ref, *refs, n_a, scale):
    a_refs, as_refs, w_refs = refs[:n_a], refs[n_a:2 * n_a], refs[2 * n_a:3 * n_a]
    res_ref, ress_ref, g_ref, x_out, xg_out, ssq_out, xs_out, xgs_out, ssqs_out = refs[3 * n_a:]
    wb = [w[...].astype(BF16) for w in w_refs]
    first_col = pl.program_id(1) == 0

    def compute(a_group, res, x_o, xg_o, ssq_o):
        acc = None
        for a, w in zip(a_group, wb):
            part = jnp.dot(a[...], w, preferred_element_type=F32)
            acc = part if acc is None else acc + part
        xn = res[...] + scale * acc
        x_o[...] = xn
        xg_o[...] = (xn * g_ref[...]).astype(BF16)
        part_ssq = _lane_fold(xn * xn)

        @pl.when(first_col)
        def _():
            ssq_o[...] = part_ssq

        @pl.when(jnp.logical_not(first_col))
        def _():
            ssq_o[...] += part_ssq

    compute(a_refs, res_ref, x_out, xg_out, ssq_out)

    @pl.when(pl.program_id(0) == 0)
    def _():
        compute(as_refs, ress_ref, xs_out, xgs_out, ssqs_out)


def _residual_matmul(lidx, a_list, as_list, w, row_blocks, res, ress, g_next, *, scale, tm, tn, name):
    m, ms = res.shape[0], ress.shape[0]
    n = w.shape[-1]
    sj = _side_col(n // tn)
    in_specs = [pl.BlockSpec((tm, a.shape[1]), lambda i, j, l: (i, 0), pipeline_mode=pl.Buffered(1))
                for a in a_list]
    in_specs += [pl.BlockSpec((ms, a.shape[1]), lambda i, j, l: (0, 0)) for a in as_list]
    in_specs += [pl.BlockSpec((None, a.shape[1], tn), functools.partial(lambda i, j, l, rb: (l[0], rb, j), rb=rb))
                 for a, rb in zip(a_list, row_blocks)]
    in_specs += [pl.BlockSpec((tm, tn), lambda i, j, l: (i, j)),
                 pl.BlockSpec((ms, tn), lambda i, j, l: (0, sj(i, j))),
                 pl.BlockSpec((None, 1, tn), lambda i, j, l: (l[1], 0, j))]
    grid_spec = pltpu.PrefetchScalarGridSpec(
        num_scalar_prefetch=1,
        grid=(m // tm, n // tn),
        in_specs=in_specs,
        out_specs=[pl.BlockSpec((tm, tn), lambda i, j, l: (i, j)),
                   pl.BlockSpec((tm, tn), lambda i, j, l: (i, j)),
                   pl.BlockSpec((tm, LANES), lambda i, j, l: (i, 0)),
                   pl.BlockSpec((ms, tn), lambda i, j, l: (0, sj(i, j))),
                   pl.BlockSpec((ms, tn), lambda i, j, l: (0, sj(i, j))),
                   pl.BlockSpec((ms, LANES), lambda i, j, l: (0, 0))],
    )
    return pl.pallas_call(
        functools.partial(_down_kernel, n_a=len(a_list), scale=scale),
        grid_spec=grid_spec,
        out_shape=[jax.ShapeDtypeStruct((m, n), F32),
                   jax.ShapeDtypeStruct((m, n), BF16),
                   jax.ShapeDtypeStruct((m, LANES), F32),
                   jax.ShapeDtypeStruct((ms, n), F32),
                   jax.ShapeDtypeStruct((ms, n), BF16),
                   jax.ShapeDtypeStruct((ms, LANES), F32)],
        compiler_params=_cparams(2),
        name=name,
    )(lidx, *a_list, *as_list, *([w] * len(a_list)), res, ress, g_next)


HIST_A = 16
HIST_B = 32
HIST_C = 8


def _mix_abc_kernel(l_ref, *refs, tt, tr, t_valid, n_prev, n_chunks, has_state, dg):
    if has_state:
        (z_ref, sp_ref, sb_ref, sc_ref, wpool_ref, pscale_ref, wdwb_ref, bdwb_ref, lng_ref, lnb_ref, wdwc_ref,
         y_ref, np_ref, nb_ref, nc_ref, full_a, full_b, full_c, shift_b, pooled_s) = refs
    else:
        (z_ref, wpool_ref, pscale_ref, wdwb_ref, bdwb_ref, lng_ref, lnb_ref, wdwc_ref,
         y_ref, np_ref, nb_ref, nc_ref, full_a, full_b, full_c, shift_b, pooled_s) = refs
    c = pl.program_id(1)

    @pl.when(c == 0)
    def _():
        full_a[0:HIST_A, :] = jnp.zeros((HIST_A, dg), F32)
        full_b[0:HIST_B, :] = jnp.zeros((HIST_B, dg), F32)
        full_c[0:HIST_C, :] = jnp.zeros((HIST_C, dg), F32)
        if has_state:
            full_a[HIST_A - POOL_BUF:HIST_A, :] = sp_ref[...]
            full_b[HIST_B - (CONV_B_WIDTH - 1):HIST_B, :] = sb_ref[...]
            full_c[HIST_C - (CONV_C_WIDTH - 1):HIST_C, :] = sc_ref[...]

    full_a[HIST_A:HIST_A + tt, :] = z_ref[:, 0:dg]
    full_b[HIST_B:HIST_B + tt, :] = z_ref[:, dg:2 * dg] * jax.nn.sigmoid(z_ref[:, 2 * dg:3 * dg])
    full_c[HIST_C:HIST_C + tt, :] = z_ref[:, 4 * dg:5 * dg] * z_ref[:, 5 * dg:6 * dg]

    n_shift = HIST_B + tt - SUBLANES
    for s in range(1, SUBLANES):
        shift_b[s - 1, :, :] = full_b[s:s + n_shift, :]

    pch = dg // len(POOL_WINDOWS)
    for r0 in range(0, tt, tr):
        t_glob = c * tt + r0 + lax.broadcasted_iota(jnp.int32, (tr, 1), 0)
        for g, w in enumerate(POOL_WINDOWS):
            cols = slice(g * pch, (g + 1) * pch)
            cur = full_a[HIST_A + r0:HIST_A + r0 + tr, cols]
            acc = cur
            for s in range(1, w):
                acc = acc + full_a[HIST_A + r0 - s:HIST_A + r0 - s + tr, cols]
            cnt = jnp.minimum(w, n_prev + t_glob + 1).astype(F32)
            pooled_s[r0:r0 + tr, cols] = acc / cnt - cur

        acc = jnp.zeros((tr, dg), F32) + bdwb_ref[...]
        for j in range(CONV_B_WIDTH):
            off = HIST_B - (CONV_B_WIDTH - 1) + j
            base, s = r0 + off - off % SUBLANES, off % SUBLANES
            src = full_b[base:base + tr, :] if s == 0 else shift_b[s - 1, base:base + tr, :]
            acc = acc + wdwb_ref[j:j + 1, :] * src
        mu = jnp.mean(acc, axis=-1, keepdims=True)
        xc = acc - mu
        var = jnp.mean(xc * xc, axis=-1, keepdims=True)
        yn = xc * lax.rsqrt(var + EPS) * lng_ref[...] + lnb_ref[...]
        y_ref[r0:r0 + tr, dg:2 * dg] = jax.nn.silu(yn).astype(y_ref.dtype)

        acc = jnp.zeros((tr, dg), F32)
        off = HIST_C - (CONV_C_WIDTH - 1) + r0
        for j in range(CONV_C_WIDTH):
            acc = acc + wdwc_ref[j:j + 1, :] * full_c[off + j:off + j + tr, :]
        y_ref[r0:r0 + tr, 2 * dg:3 * dg] = (z_ref[r0:r0 + tr, 3 * dg:4 * dg] * acc).astype(y_ref.dtype)

    for g in range(len(POOL_WINDOWS)):
        cols = slice(g * pch, (g + 1) * pch)
        ya = jnp.dot(pooled_s[:, cols].astype(BF16), wpool_ref[g].astype(BF16), preferred_element_type=F32)
        y_ref[:, cols] = (ya * pscale_ref[:, cols]).astype(y_ref.dtype)

    @pl.when(c == n_chunks - 1)
    def _():
        np_ref[...] = full_a[HIST_A + t_valid - POOL_BUF:HIST_A + t_valid, :]
        nb_ref[...] = full_b[HIST_B + t_valid - (CONV_B_WIDTH - 1):HIST_B + t_valid, :]
        nc_ref[...] = full_c[HIST_C + t_valid - (CONV_C_WIDTH - 1):HIST_C + t_valid, :]

    if n_chunks > 1:
        @pl.when(c < n_chunks - 1)
        def _():
            full_a[0:HIST_A, :] = full_a[tt:tt + HIST_A, :]
            full_b[0:HIST_B, :] = full_b[tt:tt + HIST_B, :]
            full_c[0:HIST_C, :] = full_c[tt:tt + HIST_C, :]


def _mix_abc(lidx, z, states, wts, *, tt, tr, t_valid, n_prev, out_dtype, name):
    b, t, _ = z.shape
    w_pool, pool_scale, w_dw_b, b_dw_b, ln_g, ln_b, w_dw_c = wts
    dg = pool_scale.shape[-1]
    n_chunks = t // tt
    has_state = states is not None

    def lsel(*tail):
        return lambda bi, ci, l: (l[0],) + tail

    in_specs = [pl.BlockSpec((None, tt, 6 * dg), lambda bi, ci, l: (bi, ci, 0))]
    args = [z]
    if has_state:
        for s in states:
            in_specs.append(pl.BlockSpec((None, None) + s.shape[2:], lambda bi, ci, l: (l[0], bi, 0, 0)))
            args.append(s)
    in_specs += [
        pl.BlockSpec((None,) + w_pool.shape[1:], lsel(0, 0, 0)),
        pl.BlockSpec((None, 1, dg), lsel(0, 0)),
        pl.BlockSpec((None, CONV_B_WIDTH, dg), lsel(0, 0)),
        pl.BlockSpec((None, 1, dg), lsel(0, 0)),
        pl.BlockSpec((None, 1, dg), lsel(0, 0)),
        pl.BlockSpec((None, 1, dg), lsel(0, 0)),
        pl.BlockSpec((None, CONV_C_WIDTH, dg), lsel(0, 0)),
    ]
    args += [w_pool, pool_scale, w_dw_b, b_dw_b, ln_g, ln_b, w_dw_c]
    grid_spec = pltpu.PrefetchScalarGridSpec(
        num_scalar_prefetch=1,
        grid=(b, n_chunks),
        in_specs=in_specs,
        out_specs=[pl.BlockSpec((None, tt, 3 * dg), lambda bi, ci, l: (bi, ci, 0)),
                   pl.BlockSpec((None, POOL_BUF, dg), lambda bi, ci, l: (bi, 0, 0)),
                   pl.BlockSpec((None, CONV_B_WIDTH - 1, dg), lambda bi, ci, l: (bi, 0, 0)),
                   pl.BlockSpec((None, CONV_C_WIDTH - 1, dg), lambda bi, ci, l: (bi, 0, 0))],
        scratch_shapes=[pltpu.VMEM((HIST_A + tt, dg), F32),
                        pltpu.VMEM((HIST_B + tt, dg), F32),
                        pltpu.VMEM((HIST_C + tt, dg), F32),
                        pltpu.VMEM((SUBLANES - 1, HIST_B + tt - SUBLANES, dg), F32),
                        pltpu.VMEM((tt, dg), F32)],
    )
    return pl.pallas_call(
        functools.partial(_mix_abc_kernel, tt=tt, tr=tr, t_valid=t_valid, n_prev=n_prev,
                          n_chunks=n_chunks, has_state=has_state, dg=dg),
        grid_spec=grid_spec,
        out_shape=[jax.ShapeDtypeStruct((b, t, 3 * dg), out_dtype),
                   jax.ShapeDtypeStruct((b, POOL_BUF, dg), F32),
                   jax.ShapeDtypeStruct((b, CONV_B_WIDTH - 1, dg), F32),
                   jax.ShapeDtypeStruct((b, CONV_C_WIDTH - 1, dg), F32)],
        compiler_params=_cparams(2),
        name=name,
    )(lidx, *args)


def _head_rmsnorm(x, g):
    return x * lax.rsqrt(jnp.mean(x * x, axis=-1, keepdims=True) + EPS) * g


def _dot_nt(a, b):
    return lax.dot_general(a, b, (((1,), (1,)), ((), ())), preferred_element_type=F32)


def _attn_prompt_kernel(l_ref, q_ref, k_ref, v_ref, gq_ref, gk_ref, y_ref, ko_ref, vo_ref,
                        qn_s, m_s, l_s, acc_s, qf_s, kf_s, vf_s, mf_s, lf_s, accf_s):
    t = q_ref.shape[0]
    blk = ATT_BLK
    (w0, d0), (w1, d1), (w2, d2) = DILATED_CFG
    assert d0 == 1 and d2 % d1 == 0 and w0 // d0 == blk and w1 // d1 == blk and w2 // d2 == blk
    assert (t // d2) % blk == 0
    fold, inner, tf = d1, d2 // d1, t // d1

    qn_s[...] = _head_rmsnorm(q_ref[...], gq_ref[...]) * (HEAD_DIM ** -0.5)
    ko_ref[...] = _head_rmsnorm(k_ref[...], gk_ref[...])
    vo_ref[...] = v_ref[...]

    row1 = lax.broadcasted_iota(jnp.int32, (blk, blk), 0)
    col1 = lax.broadcasted_iota(jnp.int32, (blk, blk), 1)
    causal_mask = col1 <= row1
    row2 = lax.broadcasted_iota(jnp.int32, (blk, 2 * blk), 0)
    col2 = lax.broadcasted_iota(jnp.int32, (blk, 2 * blk), 1)
    band_mask = (col2 >= row2) & (col2 <= row2 + blk)

    def rows(start, n, stride):
        return pl.ds(start, n, stride=stride) if stride > 1 else pl.ds(start, n)

    def update(qkv, state, start, stride, with_prev, first_branch):
        q_r, k_r, v_r = qkv
        m_r, l_r, acc_r = state
        qidx = rows(start, blk, stride)
        kidx = rows(start - blk * stride, 2 * blk, stride) if with_prev else qidx
        qb = q_r[qidx, :].astype(BF16)
        s = _dot_nt(qb, k_r[kidx, :].astype(BF16))
        s = jnp.where(band_mask if with_prev else causal_mask, s, NEG)
        mx = jnp.max(s, axis=-1, keepdims=True)
        vb = v_r[kidx, :].astype(BF16)
        if first_branch:
            p = jnp.exp(s - mx)
            m_r[qidx, :] = jnp.broadcast_to(mx, (blk, LANES))
            l_r[qidx, :] = jnp.broadcast_to(jnp.sum(p, axis=-1, keepdims=True), (blk, LANES))
            acc_r[qidx, :] = jnp.dot(p.astype(BF16), vb, preferred_element_type=F32)
        else:
            m_old = m_r[qidx, :]
            m_new = jnp.maximum(m_old, mx)
            alpha = jnp.exp(m_old - m_new)
            p = jnp.exp(s - (jnp.concatenate([m_new, m_new], axis=-1) if with_prev else m_new))
            m_r[qidx, :] = m_new
            l_r[qidx, :] = alpha * l_r[qidx, :] + jnp.sum(p, axis=-1, keepdims=True)
            acc_r[qidx, :] = alpha * acc_r[qidx, :] + jnp.dot(p.astype(BF16), vb, preferred_element_type=F32)

    natural = ((qn_s, ko_ref, v_ref), (m_s, l_s, acc_s))
    folded = ((qf_s, kf_s, vf_s), (mf_s, lf_s, accf_s))

    for i in range(t // blk):
        update(*natural, i * blk, 1, i > 0, True)

    for src, dst in zip(natural[0] + natural[1], folded[0] + folded[1]):
        for r in range(fold):
            dst[r * tf:(r + 1) * tf, :] = src[pl.ds(r, tf, stride=fold), :]

    for r in range(fold):
        for i in range(tf // blk):
            update(*folded, r * tf + i * blk, 1, i > 0, False)

    for r in range(fold):
        for r2 in range(inner):
            for i in range(tf // inner // blk):
                update(*folded, r * tf + r2 + i * blk * inner, inner, i > 0, False)

    accf_s[...] = accf_s[...] / lf_s[...]
    for r in range(fold):
        acc_s[pl.ds(r, tf, stride=fold), :] = accf_s[r * tf:(r + 1) * tf, :]
    y_ref[...] = acc_s[...].astype(y_ref.dtype)


def _attn_prompt(lidx, z, gq, gk, *, n_heads):
    b, t, n_in = z.shape
    dd = n_heads * HEAD_DIM
    q0 = (n_in - 3 * dd) // HEAD_DIM
    hd = HEAD_DIM
    head_out = pl.BlockSpec((None, t, hd), lambda bi, h, l: (bi, 0, h))
    grid_spec = pltpu.PrefetchScalarGridSpec(
        num_scalar_prefetch=1,
        grid=(b, n_heads),
        in_specs=[pl.BlockSpec((None, t, hd), lambda bi, h, l: (bi, 0, q0 + h)),
                  pl.BlockSpec((None, t, hd), lambda bi, h, l: (bi, 0, q0 + n_heads + h)),
                  pl.BlockSpec((None, t, hd), lambda bi, h, l: (bi, 0, q0 + 2 * n_heads + h)),
                  pl.BlockSpec((None, 1, hd), lambda bi, h, l: (l[0], 0, 0)),
                  pl.BlockSpec((None, 1, hd), lambda bi, h, l: (l[0], 0, 0))],
        out_specs=[head_out, head_out, head_out],
        scratch_shapes=[pltpu.VMEM((t, hd), F32)] * 10,
    )
    return pl.pallas_call(
        _attn_prompt_kernel,
        grid_spec=grid_spec,
        out_shape=[jax.ShapeDtypeStruct((b, t, dd), BF16),
                   jax.ShapeDtypeStruct((b, t, dd), F32),
                   jax.ShapeDtypeStruct((b, t, dd), F32)],
        compiler_params=_cparams(2),
        name="attn_prompt",
    )(lidx, z, z, z, gq, gk)


def _attn_sample_kernel(l_ref, qkv_ref, kc_ref, vc_ref, gq_ref, gk_ref, y_ref, ko_ref, *, t_valid, n_heads):
    tq = qkv_ref.shape[0]
    hd = HEAD_DIM
    lc = kc_ref.shape[0] // n_heads

    def multiplicity(dist):
        w = jnp.zeros(dist.shape, F32)
        for window, dil in DILATED_CFG:
            hit = (dist >= 0) & (dist <= window) & ((dist & (dil - 1)) == 0)
            w = w + hit.astype(F32)
        return w

    d1 = lc + lax.broadcasted_iota(jnp.int32, (tq, lc), 0) - lax.broadcasted_iota(jnp.int32, (tq, lc), 1)
    w1 = multiplicity(d1)
    d2 = lax.broadcasted_iota(jnp.int32, (tq, tq), 0) - lax.broadcasted_iota(jnp.int32, (tq, tq), 1)
    w2 = multiplicity(d2) * (lax.broadcasted_iota(jnp.int32, (tq, tq), 1) < t_valid).astype(F32)

    for h in range(n_heads):
        q = qkv_ref[:, h * hd:(h + 1) * hd]
        k = qkv_ref[:, (n_heads + h) * hd:(n_heads + h + 1) * hd]
        v = qkv_ref[:, (2 * n_heads + h) * hd:(2 * n_heads + h + 1) * hd]
        qn = (_head_rmsnorm(q, gq_ref[...]) * (HEAD_DIM ** -0.5)).astype(BF16)
        kn = _head_rmsnorm(k, gk_ref[...])
        ko_ref[:, h * hd:(h + 1) * hd] = kn
        head_rows = pl.ds(h, lc, stride=n_heads)
        s1 = jnp.where(w1 > 0, _dot_nt(qn, kc_ref[head_rows, :].astype(BF16)), NEG)
        s2 = jnp.where(w2 > 0, _dot_nt(qn, kn.astype(BF16)), NEG)
        m = jnp.maximum(jnp.max(s1, axis=-1, keepdims=True), jnp.max(s2, axis=-1, keepdims=True))
        p1 = w1 * jnp.exp(s1 - m)
        p2 = w2 * jnp.exp(s2 - m)
        den = jnp.sum(p1, axis=-1, keepdims=True) + jnp.sum(p2, axis=-1, keepdims=True)
        num = (jnp.dot(p1.astype(BF16), vc_ref[head_rows, :].astype(BF16), preferred_element_type=F32)
               + jnp.dot(p2.astype(BF16), v.astype(BF16), preferred_element_type=F32))
        y_ref[:, h * hd:(h + 1) * hd] = (num / den).astype(y_ref.dtype)


def _attn_sample(lidx, z, cache_k, cache_v, gq, gk, *, n_heads, t_valid):
    b, tq, n_in = z.shape
    depth, _, lc = cache_k.shape[:3]
    hd = HEAD_DIM
    dd = n_heads * hd
    assert (n_in - 3 * dd) % (3 * dd) == 0
    qkv_block = (n_in - 3 * dd) // (3 * dd)
    for window, dil in DILATED_CFG:
        assert dil & (dil - 1) == 0 and window <= lc
    ck = cache_k.reshape(depth, b, lc * n_heads, hd)
    cv = cache_v.reshape(depth, b, lc * n_heads, hd)
    grid_spec = pltpu.PrefetchScalarGridSpec(
        num_scalar_prefetch=1,
        grid=(b,),
        in_specs=[pl.BlockSpec((None, tq, 3 * dd), lambda bi, l: (bi, 0, qkv_block)),
                  pl.BlockSpec((None, None, lc * n_heads, hd), lambda bi, l: (l[0], bi, 0, 0)),
                  pl.BlockSpec((None, None, lc * n_heads, hd), lambda bi, l: (l[0], bi, 0, 0)),
                  pl.BlockSpec((None, 1, hd), lambda bi, l: (l[0], 0, 0)),
                  pl.BlockSpec((None, 1, hd), lambda bi, l: (l[0], 0, 0))],
        out_specs=[pl.BlockSpec((None, tq, dd), lambda bi, l: (bi, 0, 0)),
                   pl.BlockSpec((None, tq, dd), lambda bi, l: (bi, 0, 0))],
    )
    return pl.pallas_call(
        functools.partial(_attn_sample_kernel, t_valid=t_valid, n_heads=n_heads),
        grid_spec=grid_spec,
        out_shape=[jax.ShapeDtypeStruct((b, tq, dd), F32),
                   jax.ShapeDtypeStruct((b, tq, dd), F32)],
        compiler_params=_cparams(1),
        name="attn_sample",
    )(lidx, z, ck, cv, gq, gk)


SAMPLE_ROWS = 8
TM = 1024
TM_UP = 2048
TN_UP = 256
TN_WIDE = 512


def kernel(x_prompt, x_sample, state_pool, state_conv_b, state_conv_c, cache_k, cache_v, g_ffn1, w_ffn1_gate, w_ffn1_up, w_ffn1_down, g_mix, w_in, w_pool, pool_scale, w_dw_b, b_dw_b, ln_b_g, ln_b_b, w_dw_c, q_norm_g, k_norm_g, w_out, g_ffn2, w_ffn2_gate, w_ffn2_up, w_ffn2_down):
    bp, tp, d = x_prompt.shape
    bs, ts, _ = x_sample.shape
    depth = g_ffn1.shape[0]
    dg = pool_scale.shape[-1]
    n_heads = cache_k.shape[3]
    n_in = w_in.shape[-1]
    assert ts <= SAMPLE_ROWS and state_pool.shape[2] == POOL_BUF

    row = lambda a: a.reshape(depth, 1, a.shape[-1])
    g1, gm, g2 = row(g_ffn1), row(g_mix), row(g_ffn2)
    mix_w = (w_pool, row(pool_scale), w_dw_b, row(b_dw_b), row(ln_b_g), row(ln_b_b), w_dw_c)
    gq, gk = row(q_norm_g), row(k_norm_g)

    xp = x_prompt.reshape(bp * tp, d)
    xs = x_sample.reshape(bs * ts, d)
    xgp, ssqp = _prep(xp, g1[0], 256)
    xgs, ssqs = _prep(xs, g1[0], bs * ts)

    def ffn(lidx, st, wg, wu, wd, g_next, tag):
        xp, xgp, ssqp, xs, xgs, ssqs = st
        ap, a_s = _normed_matmul(lidx, xgp, ssqp, xgs, ssqs, (wg, wu), tm=min(TM_UP, bp * tp), tn=TN_UP, out_dtype=BF16,
                                 name="ffn_up" + tag)
        return tuple(_residual_matmul(lidx, [ap], [a_s], wd, [0], xp, xs, g_next, scale=0.5, tm=TM, tn=TN_UP,
                                      name="ffn_down" + tag))

    def layer_step(st, l):
        lnext = jnp.minimum(l + 1, depth - 1)
        l_same = jnp.stack([l, l]).astype(jnp.int32)
        l_next = jnp.stack([l, lnext]).astype(jnp.int32)

        st = ffn(l_same, st, w_ffn1_gate, w_ffn1_up, w_ffn1_down, gm, "1")
        xp, xgp, ssqp, xs, xgs, ssqs = st
        zp, zs = _normed_matmul(l_same, xgp, ssqp, xgs, ssqs, (w_in,), tm=min(TM_UP, bp * tp), tn=TN_WIDE, out_dtype=F32,
                                name="proj_in")

        zp = zp.reshape(bp, tp, n_in)
        yabc, pool_p, convb_p, convc_p = _mix_abc(
            l_same, zp, None, mix_w, tt=128, tr=16, t_valid=128, n_prev=0, out_dtype=BF16, name="mix_abc_p")
        yd, k_p, v_p = _attn_prompt(l_same, zp, gq, gk, n_heads=n_heads)

        zs = jnp.pad(zs.reshape(bs, ts, n_in), ((0, 0), (0, SAMPLE_ROWS - ts), (0, 0)))
        yabc_s, pool_s, convb_s, convc_s = _mix_abc(
            l_same, zs, (state_pool, state_conv_b, state_conv_c), mix_w, tt=SAMPLE_ROWS, tr=SAMPLE_ROWS,
            t_valid=ts, n_prev=POOL_BUF, out_dtype=F32, name="mix_abc_s")
        yd_s, k_s = _attn_sample(l_same, zs, cache_k, cache_v, gq, gk, n_heads=n_heads, t_valid=ts)
        v_s = zs[:, :ts, n_in - dg:]

        a_p = [yabc.reshape(bp * tp, 3 * dg), yd.reshape(bp * tp, dg)]
        a_s = [yabc_s[:, :ts].reshape(bs * ts, 3 * dg).astype(BF16), yd_s[:, :ts].reshape(bs * ts, dg).astype(BF16)]
        st = tuple(_residual_matmul(l_same, a_p, a_s, w_out, [0, 3], xp, xs, g2, scale=1.0, tm=min(TM_UP, bp * tp), tn=TN_UP,
                                    name="proj_out"))
        st = ffn(l_next, st, w_ffn2_gate, w_ffn2_up, w_ffn2_down, g1, "2")

        outs = (pool_p, pool_s, convb_p, convb_s, convc_p, convc_s,
                k_p.reshape(bp, tp, n_heads, HEAD_DIM), v_p.reshape(bp, tp, n_heads, HEAD_DIM),
                k_s[:, :ts].reshape(bs, ts, n_heads, HEAD_DIM), v_s.reshape(bs, ts, n_heads, HEAD_DIM))
        return st, outs

    st, outs = lax.scan(layer_step, (xp, xgp, ssqp, xs, xgs, ssqs), jnp.arange(depth, dtype=jnp.int32))
    return (st[0].reshape(bp, tp, d), st[3].reshape(bs, ts, d)) + tuple(outs)
```

```python
import functools

import jax
import jax.numpy as jnp
from jax import lax
from jax.experimental import pallas as pl
from jax.experimental.pallas import tpu as pltpu

EPS = 1e-6
NEG = -1e30
LANES = 128
SUBLANES = 8
HEAD_DIM = 128
POOL_WINDOWS = (2, 4, 8, 16)
POOL_BUF = max(POOL_WINDOWS) - 1
CONV_B_WIDTH = 31
CONV_C_WIDTH = 3
DILATED_CFG = ((128, 1), (512, 4), (2048, 16))
ATT_BLK = 128
VMEM_LIMIT = 60 * 1024 * 1024

F32 = jnp.float32
BF16 = jnp.bfloat16


def _cparams(n_axes):
    return pltpu.CompilerParams(dimension_semantics=("arbitrary",) * n_axes,
                                vmem_limit_bytes=VMEM_LIMIT)


def _lane_fold(v):
    n = v.shape[-1] // LANES
    out = v[:, 0:LANES]
    for k in range(1, n):
        out = out + v[:, k * LANES:(k + 1) * LANES]
    return out


def _row_rms_scale(ssq, d_model):
    return lax.rsqrt(jnp.sum(ssq, axis=-1, keepdims=True) * (1.0 / d_model) + EPS)


def _prep_kernel(x_ref, g_ref, xg_ref, ssq_ref):
    x = x_ref[...]
    xg_ref[...] = (x * g_ref[...]).astype(BF16)
    ssq_ref[...] = _lane_fold(x * x)


def _prep(x, g_row, tm):
    m, d = x.shape
    return pl.pallas_call(
        _prep_kernel,
        grid=(m // tm,),
        in_specs=[pl.BlockSpec((tm, d), lambda i: (i, 0)),
                  pl.BlockSpec((1, d), lambda i: (0, 0))],
        out_specs=[pl.BlockSpec((tm, d), lambda i: (i, 0)),
                   pl.BlockSpec((tm, LANES), lambda i: (i, 0))],
        out_shape=[jax.ShapeDtypeStruct((m, d), BF16),
                   jax.ShapeDtypeStruct((m, LANES), F32)],
        compiler_params=_cparams(1),
        name="norm_prep",
    )(x, g_row)


def _side_col(nj):
    return lambda i, j: jnp.where(i == 0, j, nj - 1)


def _up_kernel(l_ref, xg_ref, ssq_ref, xgs_ref, ssqs_ref, *rest, n_w, d_model):
    w_refs, (o_ref, os_ref, lhs_s) = rest[:n_w], rest[n_w:]
    tm = xg_ref.shape[0]
    i, j = pl.program_id(0), pl.program_id(1)
    wb = [w[...].astype(BF16) for w in w_refs]

    def act(outs):
        return outs[0] if n_w == 1 else jax.nn.silu(outs[0]) * outs[1]

    @pl.when((i == 0) & (j == 0))
    def _():
        lhs_s[0:tm, :] = xg_ref[...]
        lhs_s[tm:, :] = xgs_ref[...]

    @pl.when(i == 0)
    def _():
        r = _row_rms_scale(ssq_ref[...], d_model)
        rs = _row_rms_scale(ssqs_ref[...], d_model)
        lhs = lhs_s[...]
        outs = [jnp.dot(lhs, w, preferred_element_type=F32) for w in wb]
        o_ref[...] = act([o[0:tm] * r for o in outs]).astype(o_ref.dtype)
        os_ref[...] = act([o[tm:] * rs for o in outs]).astype(os_ref.dtype)

    @pl.when(i > 0)
    def _():
        r = _row_rms_scale(ssq_ref[...], d_model)
        xg = xg_ref[...]
        o_ref[...] = act([jnp.dot(xg, w, preferred_element_type=F32) * r for w in wb]).astype(o_ref.dtype)


def _normed_matmul(lidx, xg, ssq, xgs, ssqs, ws, *, tm, tn, out_dtype, name):
    m, k = xg.shape
    ms = xgs.shape[0]
    n = ws[0].shape[-1]
    sj = _side_col(n // tn)
    grid_spec = pltpu.PrefetchScalarGridSpec(
        num_scalar_prefetch=1,
        grid=(m // tm, n // tn),
        in_specs=[pl.BlockSpec((tm, k), lambda i, j, l: (i, 0)),
                  pl.BlockSpec((tm, LANES), lambda i, j, l: (i, 0)),
                  pl.BlockSpec((ms, k), lambda i, j, l: (0, 0)),
                  pl.BlockSpec((ms, LANES), lambda i, j, l: (0, 0))]
                 + [pl.BlockSpec((None, k, tn), lambda i, j, l: (l[0], 0, j)) for _ in ws],
        out_specs=[pl.BlockSpec((tm, tn), lambda i, j, l: (i, j)),
                   pl.BlockSpec((ms, tn), lambda i, j, l: (0, sj(i, j)))],
        scratch_shapes=[pltpu.VMEM((tm + ms, k), BF16)],
    )
    return pl.pallas_call(
        functools.partial(_up_kernel, n_w=len(ws), d_model=k),
        grid_spec=grid_spec,
        out_shape=[jax.ShapeDtypeStruct((m, n), out_dtype),
                   jax.ShapeDtypeStruct((ms, n), out_dtype)],
        compiler_params=_cparams(2),
        name=name,
    )(lidx, xg, ssq, xgs, ssqs, *ws)


def _down_kernel(l_ref, *refs, n_a, scale, stacked):
    a_refs, as_refs, w_refs = refs[:n_a], refs[n_a:2 * n_a], refs[2 * n_a:3 * n_a]
    res_ref, ress_ref, g_ref, x_out, xg_out, ssq_out, xs_out, xgs_out, ssqs_out = refs[3 * n_a:3 * n_a + 9]
    lhs_refs = refs[3 * n_a + 9:]
    tm = res_ref.shape[0]
    i, j = pl.program_id(0), pl.program_id(1)
    wb = [w[...].astype(BF16) for w in w_refs]
    first_col = j == 0

    def matmul(lhs_group):
        acc = None
        for a, w in zip(lhs_group, wb):
            part = jnp.dot(a[...], w, preferred_element_type=F32)
            acc = part if acc is None else acc + part
        return acc

    def finish(acc, res, x_o, xg_o, ssq_o):
        xn = res[...] + scale * acc
        x_o[...] = xn
        xg_o[...] = (xn * g_ref[...]).astype(BF16)
        part_ssq = _lane_fold(xn * xn)

        @pl.when(first_col)
        def _():
            ssq_o[...] = part_ssq

        @pl.when(jnp.logical_not(first_col))
        def _():
            ssq_o[...] += part_ssq

    if not stacked:
        finish(matmul(a_refs), res_ref, x_out, xg_out, ssq_out)

        @pl.when(i == 0)
        def _():
            finish(matmul(as_refs), ress_ref, xs_out, xgs_out, ssqs_out)
        return

    @pl.when((i == 0) & first_col)
    def _():
        for lhs, a, a_s in zip(lhs_refs, a_refs, as_refs):
            lhs[0:tm, :] = a[...]
            lhs[tm:, :] = a_s[...]

    @pl.when(i == 0)
    def _():
        acc = matmul(lhs_refs)
        finish(acc[0:tm], res_ref, x_out, xg_out, ssq_out)
        finish(acc[tm:], ress_ref, xs_out, xgs_out, ssqs_out)

    @pl.when(i > 0)
    def _():
        finish(matmul(a_refs), res_ref, x_out, xg_out, ssq_out)


def _residual_matmul(lidx, a_list, as_list, w, row_blocks, res, ress, g_next, *, scale, tm, tn, stacked, name):
    m, ms = res.shape[0], ress.shape[0]
    n = w.shape[-1]
    sj = _side_col(n // tn)
    in_specs = [pl.BlockSpec((tm, a.shape[1]), lambda i, j, l: (i, 0), pipeline_mode=pl.Buffered(1))
                for a in a_list]
    in_specs += [pl.BlockSpec((ms, a.shape[1]), lambda i, j, l: (0, 0)) for a in as_list]
    in_specs += [pl.BlockSpec((None, a.shape[1], tn), functools.partial(lambda i, j, l, rb: (l[0], rb, j), rb=rb))
                 for a, rb in zip(a_list, row_blocks)]
    in_specs += [pl.BlockSpec((tm, tn), lambda i, j, l: (i, j)),
                 pl.BlockSpec((ms, tn), lambda i, j, l: (0, sj(i, j))),
                 pl.BlockSpec((None, 1, tn), lambda i, j, l: (l[1], 0, j))]
    grid_spec = pltpu.PrefetchScalarGridSpec(
        num_scalar_prefetch=1,
        grid=(m // tm, n // tn),
        in_specs=in_specs,
        out_specs=[pl.BlockSpec((tm, tn), lambda i, j, l: (i, j)),
                   pl.BlockSpec((tm, tn), lambda i, j, l: (i, j)),
                   pl.BlockSpec((tm, LANES), lambda i, j, l: (i, 0)),
                   pl.BlockSpec((ms, tn), lambda i, j, l: (0, sj(i, j))),
                   pl.BlockSpec((ms, tn), lambda i, j, l: (0, sj(i, j))),
                   pl.BlockSpec((ms, LANES), lambda i, j, l: (0, 0))],
        scratch_shapes=[pltpu.VMEM((tm + ms, a.shape[1]), BF16) for a in a_list] if stacked else [],
    )
    return pl.pallas_call(
        functools.partial(_down_kernel, n_a=len(a_list), scale=scale, stacked=stacked),
        grid_spec=grid_spec,
        out_shape=[jax.ShapeDtypeStruct((m, n), F32),
                   jax.ShapeDtypeStruct((m, n), BF16),
                   jax.ShapeDtypeStruct((m, LANES), F32),
                   jax.ShapeDtypeStruct((ms, n), F32),
                   jax.ShapeDtypeStruct((ms, n), BF16),
                   jax.ShapeDtypeStruct((ms, LANES), F32)],
        compiler_params=_cparams(2),
        name=name,
    )(lidx, *a_list, *as_list, *([w] * len(a_list)), res, ress, g_next)


HIST_A = 16
HIST_B = 32
HIST_C = 8


def _mix_abc_kernel(l_ref, *refs, tt, tr, t_valid, n_prev, n_chunks, has_state, dg):
    if has_state:
        (z_ref, sp_ref, sb_ref, sc_ref, wpool_ref, pscale_ref, wdwb_ref, bdwb_ref, lng_ref, lnb_ref, wdwc_ref,
         y_ref, np_ref, nb_ref, nc_ref, full_a, full_b, full_c, shift_b, pooled_s) = refs
    else:
        (z_ref, wpool_ref, pscale_ref, wdwb_ref, bdwb_ref, lng_ref, lnb_ref, wdwc_ref,
         y_ref, np_ref, nb_ref, nc_ref, full_a, full_b, full_c, shift_b, pooled_s) = refs
    c = pl.program_id(1)

    @pl.when(c == 0)
    def _():
        full_a[0:HIST_A, :] = jnp.zeros((HIST_A, dg), F32)
        full_b[0:HIST_B, :] = jnp.zeros((HIST_B, dg), F32)
        full_c[0:HIST_C, :] = jnp.zeros((HIST_C, dg), F32)
        if has_state:
            full_a[HIST_A - POOL_BUF:HIST_A, :] = sp_ref[...]
            full_b[HIST_B - (CONV_B_WIDTH - 1):HIST_B, :] = sb_ref[...]
            full_c[HIST_C - (CONV_C_WIDTH - 1):HIST_C, :] = sc_ref[...]

    full_a[HIST_A:HIST_A + tt, :] = z_ref[:, 0:dg]
    full_b[HIST_B:HIST_B + tt, :] = z_ref[:, dg:2 * dg] * jax.nn.sigmoid(z_ref[:, 2 * dg:3 * dg])
    full_c[HIST_C:HIST_C + tt, :] = z_ref[:, 4 * dg:5 * dg] * z_ref[:, 5 * dg:6 * dg]

    n_shift = HIST_B + tt - SUBLANES
    for s in range(1, SUBLANES):
        shift_b[s - 1, :, :] = full_b[s:s + n_shift, :]

    def tap(w_ref, j):
        w = w_ref[j]
        return w if tr == SUBLANES else jnp.concatenate([w] * (tr // SUBLANES), axis=0)

    pch = dg // len(POOL_WINDOWS)
    for r0 in range(0, tt, tr):
        t_glob = c * tt + r0 + lax.broadcasted_iota(jnp.int32, (tr, 1), 0)
        for g, w in enumerate(POOL_WINDOWS):
            cols = slice(g * pch, (g + 1) * pch)
            cur = full_a[HIST_A + r0:HIST_A + r0 + tr, cols]
            acc = cur
            for s in range(1, w):
                acc = acc + full_a[HIST_A + r0 - s:HIST_A + r0 - s + tr, cols]
            cnt = jnp.minimum(w, n_prev + t_glob + 1).astype(F32)
            pooled_s[r0:r0 + tr, cols] = acc / cnt - cur

        acc = jnp.zeros((tr, dg), F32) + bdwb_ref[...]
        for j in range(CONV_B_WIDTH):
            off = HIST_B - (CONV_B_WIDTH - 1) + j
            base, s = r0 + off - off % SUBLANES, off % SUBLANES
            src = full_b[base:base + tr, :] if s == 0 else shift_b[s - 1, base:base + tr, :]
            acc = acc + tap(wdwb_ref, j) * src
        mu = jnp.mean(acc, axis=-1, keepdims=True)
        xc = acc - mu
        var = jnp.mean(xc * xc, axis=-1, keepdims=True)
        yn = xc * lax.rsqrt(var + EPS) * lng_ref[...] + lnb_ref[...]
        y_ref[r0:r0 + tr, dg:2 * dg] = jax.nn.silu(yn).astype(y_ref.dtype)

        acc = jnp.zeros((tr, dg), F32)
        off = HIST_C - (CONV_C_WIDTH - 1) + r0
        for j in range(CONV_C_WIDTH):
            acc = acc + tap(wdwc_ref, j) * full_c[off + j:off + j + tr, :]
        y_ref[r0:r0 + tr, 2 * dg:3 * dg] = (z_ref[r0:r0 + tr, 3 * dg:4 * dg] * acc).astype(y_ref.dtype)

    for g in range(len(POOL_WINDOWS)):
        cols = slice(g * pch, (g + 1) * pch)
        ya = jnp.dot(pooled_s[:, cols].astype(BF16), wpool_ref[g].astype(BF16), preferred_element_type=F32)
        y_ref[:, cols] = (ya * pscale_ref[:, cols]).astype(y_ref.dtype)

    @pl.when(c == n_chunks - 1)
    def _():
        np_ref[...] = full_a[HIST_A + t_valid - POOL_BUF:HIST_A + t_valid, :]
        nb_ref[...] = full_b[HIST_B + t_valid - (CONV_B_WIDTH - 1):HIST_B + t_valid, :]
        nc_ref[...] = full_c[HIST_C + t_valid - (CONV_C_WIDTH - 1):HIST_C + t_valid, :]

    if n_chunks > 1:
        @pl.when(c < n_chunks - 1)
        def _():
            full_a[0:HIST_A, :] = full_a[tt:tt + HIST_A, :]
            full_b[0:HIST_B, :] = full_b[tt:tt + HIST_B, :]
            full_c[0:HIST_C, :] = full_c[tt:tt + HIST_C, :]


def _mix_abc(lidx, z, states, wts, *, tt, tr, t_valid, n_prev, out_dtype, name):
    b, t, _ = z.shape
    w_pool, pool_scale, w_dw_b, b_dw_b, ln_g, ln_b, w_dw_c = wts
    dg = pool_scale.shape[-1]
    n_chunks = t // tt
    has_state = states is not None

    def lsel(*tail):
        return lambda bi, ci, l: (l[0],) + tail

    in_specs = [pl.BlockSpec((None, tt, 6 * dg), lambda bi, ci, l: (bi, ci, 0))]
    args = [z]
    if has_state:
        for s in states:
            in_specs.append(pl.BlockSpec((None, None) + s.shape[2:], lambda bi, ci, l: (l[0], bi, 0, 0)))
            args.append(s)
    in_specs += [
        pl.BlockSpec((None,) + w_pool.shape[1:], lsel(0, 0, 0)),
        pl.BlockSpec((None, 1, dg), lsel(0, 0)),
        pl.BlockSpec((None, CONV_B_WIDTH, SUBLANES, dg), lsel(0, 0, 0)),
        pl.BlockSpec((None, 1, dg), lsel(0, 0)),
        pl.BlockSpec((None, 1, dg), lsel(0, 0)),
        pl.BlockSpec((None, 1, dg), lsel(0, 0)),
        pl.BlockSpec((None, CONV_C_WIDTH, SUBLANES, dg), lsel(0, 0, 0)),
    ]
    args += [w_pool, pool_scale, w_dw_b, b_dw_b, ln_g, ln_b, w_dw_c]
    grid_spec = pltpu.PrefetchScalarGridSpec(
        num_scalar_prefetch=1,
        grid=(b, n_chunks),
        in_specs=in_specs,
        out_specs=[pl.BlockSpec((None, tt, 3 * dg), lambda bi, ci, l: (bi, ci, 0)),
                   pl.BlockSpec((None, POOL_BUF, dg), lambda bi, ci, l: (bi, 0, 0)),
                   pl.BlockSpec((None, CONV_B_WIDTH - 1, dg), lambda bi, ci, l: (bi, 0, 0)),
                   pl.BlockSpec((None, CONV_C_WIDTH - 1, dg), lambda bi, ci, l: (bi, 0, 0))],
        scratch_shapes=[pltpu.VMEM((HIST_A + tt, dg), F32),
                        pltpu.VMEM((HIST_B + tt, dg), F32),
                        pltpu.VMEM((HIST_C + tt, dg), F32),
                        pltpu.VMEM((SUBLANES - 1, HIST_B + tt - SUBLANES, dg), F32),
                        pltpu.VMEM((tt, dg), F32)],
    )
    return pl.pallas_call(
        functools.partial(_mix_abc_kernel, tt=tt, tr=tr, t_valid=t_valid, n_prev=n_prev,
                          n_chunks=n_chunks, has_state=has_state, dg=dg),
        grid_spec=grid_spec,
        out_shape=[jax.ShapeDtypeStruct((b, t, 3 * dg), out_dtype),
                   jax.ShapeDtypeStruct((b, POOL_BUF, dg), F32),
                   jax.ShapeDtypeStruct((b, CONV_B_WIDTH - 1, dg), F32),
                   jax.ShapeDtypeStruct((b, CONV_C_WIDTH - 1, dg), F32)],
        compiler_params=_cparams(2),
        name=name,
    )(lidx, *args)


def _head_rmsnorm(x, g):
    return x * lax.rsqrt(jnp.mean(x * x, axis=-1, keepdims=True) + EPS) * g


def _dot_nt(a, b):
    return lax.dot_general(a, b, (((1,), (1,)), ((), ())), preferred_element_type=F32)


def _attn_prompt_kernel(l_ref, q_ref, k_ref, v_ref, gq_ref, gk_ref, y_ref, ko_ref, vo_ref,
                        qn_s, m_s, l_s, acc_s, qf_s, kf_s, vf_s, mf_s, lf_s, accf_s):
    t = q_ref.shape[0]
    blk = ATT_BLK
    (w0, d0), (w1, d1), (w2, d2) = DILATED_CFG
    assert d0 == 1 and d2 % d1 == 0 and w0 // d0 == blk and w1 // d1 == blk and w2 // d2 == blk
    assert (t // d2) % blk == 0
    fold, inner, tf = d1, d2 // d1, t // d1

    qn_s[...] = _head_rmsnorm(q_ref[...], gq_ref[...]) * (HEAD_DIM ** -0.5)
    ko_ref[...] = _head_rmsnorm(k_ref[...], gk_ref[...])
    vo_ref[...] = v_ref[...]

    row1 = lax.broadcasted_iota(jnp.int32, (blk, blk), 0)
    col1 = lax.broadcasted_iota(jnp.int32, (blk, blk), 1)
    causal_mask = col1 <= row1
    row2 = lax.broadcasted_iota(jnp.int32, (blk, 2 * blk), 0)
    col2 = lax.broadcasted_iota(jnp.int32, (blk, 2 * blk), 1)
    band_mask = (col2 >= row2) & (col2 <= row2 + blk)

    def rows(start, n, stride):
        return pl.ds(start, n, stride=stride) if stride > 1 else pl.ds(start, n)

    def update(qkv, state, start, stride, with_prev, first_branch):
        q_r, k_r, v_r = qkv
        m_r, l_r, acc_r = state
        qidx = rows(start, blk, stride)
        kidx = rows(start - blk * stride, 2 * blk, stride) if with_prev else qidx
        qb = q_r[qidx, :].astype(BF16)
        s = _dot_nt(qb, k_r[kidx, :].astype(BF16))
        s = jnp.where(band_mask if with_prev else causal_mask, s, NEG)
        mx = jnp.max(s, axis=-1, keepdims=True)
        vb = v_r[kidx, :].astype(BF16)
        if first_branch:
            p = jnp.exp(s - mx)
            m_r[qidx, :] = jnp.broadcast_to(mx, (blk, LANES))
            l_r[qidx, :] = jnp.broadcast_to(jnp.sum(p, axis=-1, keepdims=True), (blk, LANES))
            acc_r[qidx, :] = jnp.dot(p.astype(BF16), vb, preferred_element_type=F32)
        else:
            m_old = m_r[qidx, :]
            m_new = jnp.maximum(m_old, mx)
            alpha = jnp.exp(m_old - m_new)
            p = jnp.exp(s - (jnp.concatenate([m_new, m_new], axis=-1) if with_prev else m_new))
            m_r[qidx, :] = m_new
            l_r[qidx, :] = alpha * l_r[qidx, :] + jnp.sum(p, axis=-1, keepdims=True)
            acc_r[qidx, :] = alpha * acc_r[qidx, :] + jnp.dot(p.astype(BF16), vb, preferred_element_type=F32)

    natural = ((qn_s, ko_ref, v_ref), (m_s, l_s, acc_s))
    folded = ((qf_s, kf_s, vf_s), (mf_s, lf_s, accf_s))

    for i in range(t // blk):
        update(*natural, i * blk, 1, i > 0, True)

    for src, dst in zip(natural[0] + natural[1], folded[0] + folded[1]):
        for r in range(fold):
            dst[r * tf:(r + 1) * tf, :] = src[pl.ds(r, tf, stride=fold), :]

    for r in range(fold):
        for i in range(tf // blk):
            update(*folded, r * tf + i * blk, 1, i > 0, False)

    for r in range(fold):
        for r2 in range(inner):
            for i in range(tf // inner // blk):
                update(*folded, r * tf + r2 + i * blk * inner, inner, i > 0, False)

    accf_s[...] = accf_s[...] / lf_s[...]
    for r in range(fold):
        acc_s[pl.ds(r, tf, stride=fold), :] = accf_s[r * tf:(r + 1) * tf, :]
    y_ref[...] = acc_s[...].astype(y_ref.dtype)


def _attn_prompt(lidx, z, gq, gk, *, n_heads):
    b, t, n_in = z.shape
    dd = n_heads * HEAD_DIM
    q0 = (n_in - 3 * dd) // HEAD_DIM
    hd = HEAD_DIM
    head_out = pl.BlockSpec((None, t, hd), lambda bi, h, l: (bi, 0, h))
    grid_spec = pltpu.PrefetchScalarGridSpec(
        num_scalar_prefetch=1,
        grid=(b, n_heads),
        in_specs=[pl.BlockSpec((None, t, hd), lambda bi, h, l: (bi, 0, q0 + h)),
                  pl.BlockSpec((None, t, hd), lambda bi, h, l: (bi, 0, q0 + n_heads + h)),
                  pl.BlockSpec((None, t, hd), lambda bi, h, l: (bi, 0, q0 + 2 * n_heads + h)),
                  pl.BlockSpec((None, 1, hd), lambda bi, h, l: (l[0], 0, 0)),
                  pl.BlockSpec((None, 1, hd), lambda bi, h, l: (l[0], 0, 0))],
        out_specs=[head_out, head_out, head_out],
        scratch_shapes=[pltpu.VMEM((t, hd), F32)] * 10,
    )
    return pl.pallas_call(
        _attn_prompt_kernel,
        grid_spec=grid_spec,
        out_shape=[jax.ShapeDtypeStruct((b, t, dd), BF16),
                   jax.ShapeDtypeStruct((b, t, dd), F32),
                   jax.ShapeDtypeStruct((b, t, dd), F32)],
        compiler_params=_cparams(2),
        name="attn_prompt",
    )(lidx, z, z, z, gq, gk)


def _attn_sample_kernel(l_ref, qkv_ref, kc_ref, vc_ref, gq_ref, gk_ref, y_ref, ko_ref, *, t_valid, n_heads):
    tq = qkv_ref.shape[0]
    hd = HEAD_DIM
    lc = kc_ref.shape[0] // n_heads

    def multiplicity(dist):
        w = jnp.zeros(dist.shape, F32)
        for window, dil in DILATED_CFG:
            hit = (dist >= 0) & (dist <= window) & ((dist & (dil - 1)) == 0)
            w = w + hit.astype(F32)
        return w

    d1 = lc + lax.broadcasted_iota(jnp.int32, (tq, lc), 0) - lax.broadcasted_iota(jnp.int32, (tq, lc), 1)
    w1 = multiplicity(d1)
    d2 = lax.broadcasted_iota(jnp.int32, (tq, tq), 0) - lax.broadcasted_iota(jnp.int32, (tq, tq), 1)
    w2 = multiplicity(d2) * (lax.broadcasted_iota(jnp.int32, (tq, tq), 1) < t_valid).astype(F32)

    for h in range(n_heads):
        q = qkv_ref[:, h * hd:(h + 1) * hd]
        k = qkv_ref[:, (n_heads + h) * hd:(n_heads + h + 1) * hd]
        v = qkv_ref[:, (2 * n_heads + h) * hd:(2 * n_heads + h + 1) * hd]
        qn = (_head_rmsnorm(q, gq_ref[...]) * (HEAD_DIM ** -0.5)).astype(BF16)
        kn = _head_rmsnorm(k, gk_ref[...])
        ko_ref[:, h * hd:(h + 1) * hd] = kn
        head_rows = pl.ds(h, lc, stride=n_heads)
        s1 = jnp.where(w1 > 0, _dot_nt(qn, kc_ref[head_rows, :].astype(BF16)), NEG)
        s2 = jnp.where(w2 > 0, _dot_nt(qn, kn.astype(BF16)), NEG)
        m = jnp.maximum(jnp.max(s1, axis=-1, keepdims=True), jnp.max(s2, axis=-1, keepdims=True))
        p1 = w1 * jnp.exp(s1 - m)
        p2 = w2 * jnp.exp(s2 - m)
        den = jnp.sum(p1, axis=-1, keepdims=True) + jnp.sum(p2, axis=-1, keepdims=True)
        num = (jnp.dot(p1.astype(BF16), vc_ref[head_rows, :].astype(BF16), preferred_element_type=F32)
               + jnp.dot(p2.astype(BF16), v.astype(BF16), preferred_element_type=F32))
        y_ref[:, h * hd:(h + 1) * hd] = (num / den).astype(y_ref.dtype)


def _attn_sample(lidx, z, cache_k, cache_v, gq, gk, *, n_heads, t_valid):
    b, tq, n_in = z.shape
    depth, _, lc = cache_k.shape[:3]
    hd = HEAD_DIM
    dd = n_heads * hd
    assert (n_in - 3 * dd) % (3 * dd) == 0
    qkv_block = (n_in - 3 * dd) // (3 * dd)
    for window, dil in DILATED_CFG:
        assert dil & (dil - 1) == 0 and window <= lc
    ck = cache_k.reshape(depth, b, lc * n_heads, hd)
    cv = cache_v.reshape(depth, b, lc * n_heads, hd)
    grid_spec = pltpu.PrefetchScalarGridSpec(
        num_scalar_prefetch=1,
        grid=(b,),
        in_specs=[pl.BlockSpec((None, tq, 3 * dd), lambda bi, l: (bi, 0, qkv_block)),
                  pl.BlockSpec((None, None, lc * n_heads, hd), lambda bi, l: (l[0], bi, 0, 0)),
                  pl.BlockSpec((None, None, lc * n_heads, hd), lambda bi, l: (l[0], bi, 0, 0)),
                  pl.BlockSpec((None, 1, hd), lambda bi, l: (l[0], 0, 0)),
                  pl.BlockSpec((None, 1, hd), lambda bi, l: (l[0], 0, 0))],
        out_specs=[pl.BlockSpec((None, tq, dd), lambda bi, l: (bi, 0, 0)),
                   pl.BlockSpec((None, tq, dd), lambda bi, l: (bi, 0, 0))],
    )
    return pl.pallas_call(
        functools.partial(_attn_sample_kernel, t_valid=t_valid, n_heads=n_heads),
        grid_spec=grid_spec,
        out_shape=[jax.ShapeDtypeStruct((b, tq, dd), F32),
                   jax.ShapeDtypeStruct((b, tq, dd), F32)],
        compiler_params=_cparams(1),
        name="attn_sample",
    )(lidx, z, ck, cv, gq, gk)


SAMPLE_ROWS = 8
TM = 1024
TN_UP = 256
TN_WIDE = 512


def kernel(x_prompt, x_sample, state_pool, state_conv_b, state_conv_c, cache_k, cache_v, g_ffn1, w_ffn1_gate, w_ffn1_up, w_ffn1_down, g_mix, w_in, w_pool, pool_scale, w_dw_b, b_dw_b, ln_b_g, ln_b_b, w_dw_c, q_norm_g, k_norm_g, w_out, g_ffn2, w_ffn2_gate, w_ffn2_up, w_ffn2_down):
    bp, tp, d = x_prompt.shape
    bs, ts, _ = x_sample.shape
    depth = g_ffn1.shape[0]
    dg = pool_scale.shape[-1]
    n_heads = cache_k.shape[3]
    n_in = w_in.shape[-1]
    assert ts <= SAMPLE_ROWS and state_pool.shape[2] == POOL_BUF

    row = lambda a: a.reshape(depth, 1, a.shape[-1])
    g1, gm, g2 = row(g_ffn1), row(g_mix), row(g_ffn2)
    taps = lambda a: jnp.broadcast_to(a[:, :, None, :], a.shape[:2] + (SUBLANES, a.shape[-1]))
    mix_w = (w_pool, row(pool_scale), taps(w_dw_b), row(b_dw_b), row(ln_b_g), row(ln_b_b), taps(w_dw_c))
    gq, gk = row(q_norm_g), row(k_norm_g)

    xp = x_prompt.reshape(bp * tp, d)
    xs = x_sample.reshape(bs * ts, d)
    xgp, ssqp = _prep(xp, g1[0], 256)
    xgs, ssqs = _prep(xs, g1[0], bs * ts)

    def ffn(lidx, st, wg, wu, wd, g_next, tag):
        xp, xgp, ssqp, xs, xgs, ssqs = st
        ap, a_s = _normed_matmul(lidx, xgp, ssqp, xgs, ssqs, (wg, wu), tm=TM, tn=TN_UP, out_dtype=BF16,
                                 name="ffn_up" + tag)
        return tuple(_residual_matmul(lidx, [ap], [a_s], wd, [0], xp, xs, g_next, scale=0.5, tm=TM, tn=TN_UP,
                                      stacked=False, name="ffn_down" + tag))

    def layer_step(st, l):
        lnext = jnp.minimum(l + 1, depth - 1)
        l_same = jnp.stack([l, l]).astype(jnp.int32)
        l_next = jnp.stack([l, lnext]).astype(jnp.int32)

        st = ffn(l_same, st, w_ffn1_gate, w_ffn1_up, w_ffn1_down, gm, "1")
        xp, xgp, ssqp, xs, xgs, ssqs = st
        zp, zs = _normed_matmul(l_same, xgp, ssqp, xgs, ssqs, (w_in,), tm=TM, tn=TN_WIDE, out_dtype=F32,
                                name="proj_in")

        zp = zp.reshape(bp, tp, n_in)
        yabc, pool_p, convb_p, convc_p = _mix_abc(
            l_same, zp, None, mix_w, tt=128, tr=16, t_valid=128, n_prev=0, out_dtype=BF16, name="mix_abc_p")
        yd, k_p, v_p = _attn_prompt(l_same, zp, gq, gk, n_heads=n_heads)

        zs = jnp.pad(zs.reshape(bs, ts, n_in), ((0, 0), (0, SAMPLE_ROWS - ts), (0, 0)))
        yabc_s, pool_s, convb_s, convc_s = _mix_abc(
            l_same, zs, (state_pool, state_conv_b, state_conv_c), mix_w, tt=SAMPLE_ROWS, tr=SAMPLE_ROWS,
            t_valid=ts, n_prev=POOL_BUF, out_dtype=F32, name="mix_abc_s")
        yd_s, k_s = _attn_sample(l_same, zs, cache_k, cache_v, gq, gk, n_heads=n_heads, t_valid=ts)
        v_s = zs[:, :ts, n_in - dg:]

        a_p = [yabc.reshape(bp * tp, 3 * dg), yd.reshape(bp * tp, dg)]
        a_s = [yabc_s[:, :ts].reshape(bs * ts, 3 * dg).astype(BF16), yd_s[:, :ts].reshape(bs * ts, dg).astype(BF16)]
        st = tuple(_residual_matmul(l_same, a_p, a_s, w_out, [0, 3], xp, xs, g2, scale=1.0, tm=TM, tn=TN_WIDE,
                                    stacked=True, name="proj_out"))
        st = ffn(l_next, st, w_ffn2_gate, w_ffn2_up, w_ffn2_down, g1, "2")

        outs = (pool_p, pool_s, convb_p, convb_s, convc_p, convc_s,
                k_p.reshape(bp, tp, n_heads, HEAD_DIM), v_p.reshape(bp, tp, n_heads, HEAD_DIM),
                k_s[:, :ts].reshape(bs, ts, n_heads, HEAD_DIM), v_s.reshape(bs, ts, n_heads, HEAD_DIM))
        return st, outs

    st, outs = lax.scan(layer_step, (xp, xgp, ssqp, xs, xgs, ssqs), jnp.arange(depth, dtype=jnp.int32))
    return (st[0].reshape(bp, tp, d), st[3].reshape(bs, ts, d)) + tuple(outs)
```

```python
import functools

import jax
import jax.numpy as jnp
from jax import lax
from jax.experimental import pallas as pl
from jax.experimental.pallas import tpu as pltpu

EPS = 1e-6
NEG = -1e30
LANES = 128
SUBLANES = 8
HEAD_DIM = 128
POOL_WINDOWS = (2, 4, 8, 16)
POOL_BUF = max(POOL_WINDOWS) - 1
CONV_B_WIDTH = 31
CONV_C_WIDTH = 3
DILATED_CFG = ((128, 1), (512, 4), (2048, 16))
ATT_BLK = 128
VMEM_LIMIT = 60 * 1024 * 1024

F32 = jnp.float32
BF16 = jnp.bfloat16


def _cparams(n_axes):
    return pltpu.CompilerParams(dimension_semantics=("arbitrary",) * n_axes,
                                vmem_limit_bytes=VMEM_LIMIT)


def _lane_fold(v):
    n = v.shape[-1] // LANES
    out = v[:, 0:LANES]
    for k in range(1, n):
        out = out + v[:, k * LANES:(k + 1) * LANES]
    return out


def _row_rms_scale(ssq, d_model):
    return lax.rsqrt(jnp.sum(ssq, axis=-1, keepdims=True) * (1.0 / d_model) + EPS)


def _prep_kernel(x_ref, g_ref, xg_ref, ssq_ref):
    x = x_ref[...]
    xg_ref[...] = (x * g_ref[...]).astype(BF16)
    ssq_ref[...] = _lane_fold(x * x)


def _prep(x, g_row, tm):
    m, d = x.shape
    return pl.pallas_call(
        _prep_kernel,
        grid=(m // tm,),
        in_specs=[pl.BlockSpec((tm, d), lambda i: (i, 0)),
                  pl.BlockSpec((1, d), lambda i: (0, 0))],
        out_specs=[pl.BlockSpec((tm, d), lambda i: (i, 0)),
                   pl.BlockSpec((tm, LANES), lambda i: (i, 0))],
        out_shape=[jax.ShapeDtypeStruct((m, d), BF16),
                   jax.ShapeDtypeStruct((m, LANES), F32)],
        compiler_params=_cparams(1),
        name="norm_prep",
    )(x, g_row)


def _side_col(nj):
    return lambda i, j: jnp.where(i == 0, j, nj - 1)


def _up_kernel(l_ref, xg_ref, ssq_ref, xgs_ref, ssqs_ref, *rest, n_w, d_model):
    w_refs, (o_ref, os_ref, lhs_s) = rest[:n_w], rest[n_w:]
    tm = xg_ref.shape[0]
    i, j = pl.program_id(0), pl.program_id(1)

    def weights():
        return [w[...].astype(BF16) for w in w_refs]

    def act(outs):
        return outs[0] if n_w == 1 else jax.nn.silu(outs[0]) * outs[1]

    @pl.when((i == 0) & (j == 0))
    def _():
        lhs_s[0:tm, :] = xg_ref[...]
        lhs_s[tm:, :] = xgs_ref[...]

    @pl.when(i == 0)
    def _():
        r = _row_rms_scale(ssq_ref[...], d_model)
        rs = _row_rms_scale(ssqs_ref[...], d_model)
        lhs = lhs_s[...]
        outs = [jnp.dot(lhs, w, preferred_element_type=F32) for w in weights()]
        o_ref[...] = act([o[0:tm] * r for o in outs]).astype(o_ref.dtype)
        os_ref[...] = act([o[tm:] * rs for o in outs]).astype(os_ref.dtype)

    @pl.when(i > 0)
    def _():
        r = _row_rms_scale(ssq_ref[...], d_model)
        xg = xg_ref[...]
        o_ref[...] = act([jnp.dot(xg, w, preferred_element_type=F32) * r for w in weights()]).astype(o_ref.dtype)


def _normed_matmul(lidx, xg, ssq, xgs, ssqs, ws, *, tm, tn, out_dtype, name):
    m, k = xg.shape
    ms = xgs.shape[0]
    n = ws[0].shape[-1]
    sj = _side_col(n // tn)
    grid_spec = pltpu.PrefetchScalarGridSpec(
        num_scalar_prefetch=1,
        grid=(m // tm, n // tn),
        in_specs=[pl.BlockSpec((tm, k), lambda i, j, l: (i, 0)),
                  pl.BlockSpec((tm, LANES), lambda i, j, l: (i, 0)),
                  pl.BlockSpec((ms, k), lambda i, j, l: (0, 0)),
                  pl.BlockSpec((ms, LANES), lambda i, j, l: (0, 0))]
                 + [pl.BlockSpec((None, k, tn), lambda i, j, l: (l[0], 0, j)) for _ in ws],
        out_specs=[pl.BlockSpec((tm, tn), lambda i, j, l: (i, j)),
                   pl.BlockSpec((ms, tn), lambda i, j, l: (0, sj(i, j)))],
        scratch_shapes=[pltpu.VMEM((tm + ms, k), BF16)],
    )
    return pl.pallas_call(
        functools.partial(_up_kernel, n_w=len(ws), d_model=k),
        grid_spec=grid_spec,
        out_shape=[jax.ShapeDtypeStruct((m, n), out_dtype),
                   jax.ShapeDtypeStruct((ms, n), out_dtype)],
        compiler_params=_cparams(2),
        name=name,
    )(lidx, xg, ssq, xgs, ssqs, *ws)


def _down_kernel(l_ref, *refs, n_a, scale, stacked):
    a_refs, as_refs, w_refs = refs[:n_a], refs[n_a:2 * n_a], refs[2 * n_a:3 * n_a]
    res_ref, ress_ref, g_ref, x_out, xg_out, ssq_out, xs_out, xgs_out, ssqs_out = refs[3 * n_a:3 * n_a + 9]
    lhs_refs = refs[3 * n_a + 9:]
    tm = res_ref.shape[0]
    i, j = pl.program_id(0), pl.program_id(1)
    first_col = j == 0

    def weights():
        return [w[...].astype(BF16) for w in w_refs]

    def matmul(lhs_group, wb):
        acc = None
        for a, w in zip(lhs_group, wb):
            part = jnp.dot(a[...], w, preferred_element_type=F32)
            acc = part if acc is None else acc + part
        return acc

    def finish(acc, res, x_o, xg_o, ssq_o):
        xn = res[...] + scale * acc
        x_o[...] = xn
        xg_o[...] = (xn * g_ref[...]).astype(BF16)
        part_ssq = _lane_fold(xn * xn)

        @pl.when(first_col)
        def _():
            ssq_o[...] = part_ssq

        @pl.when(jnp.logical_not(first_col))
        def _():
            ssq_o[...] += part_ssq

    if not stacked:
        wb = weights()
        finish(matmul(a_refs, wb), res_ref, x_out, xg_out, ssq_out)

        @pl.when(i == 0)
        def _():
            finish(matmul(as_refs, wb), ress_ref, xs_out, xgs_out, ssqs_out)
        return

    @pl.when((i == 0) & first_col)
    def _():
        for lhs, a, a_s in zip(lhs_refs, a_refs, as_refs):
            lhs[0:tm, :] = a[...]
            lhs[tm:, :] = a_s[...]

    @pl.when(i == 0)
    def _():
        acc = matmul(lhs_refs, weights())
        finish(acc[0:tm], res_ref, x_out, xg_out, ssq_out)
        finish(acc[tm:], ress_ref, xs_out, xgs_out, ssqs_out)

    @pl.when(i > 0)
    def _():
        finish(matmul(a_refs, weights()), res_ref, x_out, xg_out, ssq_out)


def _residual_matmul(lidx, a_list, as_list, w, row_blocks, res, ress, g_next, *, scale, tm, tn, stacked, name):
    m, ms = res.shape[0], ress.shape[0]
    n = w.shape[-1]
    sj = _side_col(n // tn)
    in_specs = [pl.BlockSpec((tm, a.shape[1]), lambda i, j, l: (i, 0), pipeline_mode=pl.Buffered(1))
                for a in a_list]
    in_specs += [pl.BlockSpec((ms, a.shape[1]), lambda i, j, l: (0, 0)) for a in as_list]
    in_specs += [pl.BlockSpec((None, a.shape[1], tn), functools.partial(lambda i, j, l, rb: (l[0], rb, j), rb=rb))
                 for a, rb in zip(a_list, row_blocks)]
    in_specs += [pl.BlockSpec((tm, tn), lambda i, j, l: (i, j)),
                 pl.BlockSpec((ms, tn), lambda i, j, l: (0, sj(i, j))),
                 pl.BlockSpec((None, 1, tn), lambda i, j, l: (l[1], 0, j))]
    grid_spec = pltpu.PrefetchScalarGridSpec(
        num_scalar_prefetch=1,
        grid=(m // tm, n // tn),
        in_specs=in_specs,
        out_specs=[pl.BlockSpec((tm, tn), lambda i, j, l: (i, j)),
                   pl.BlockSpec((tm, tn), lambda i, j, l: (i, j)),
                   pl.BlockSpec((tm, LANES), lambda i, j, l: (i, 0)),
                   pl.BlockSpec((ms, tn), lambda i, j, l: (0, sj(i, j))),
                   pl.BlockSpec((ms, tn), lambda i, j, l: (0, sj(i, j))),
                   pl.BlockSpec((ms, LANES), lambda i, j, l: (0, 0))],
        scratch_shapes=[pltpu.VMEM((tm + ms, a.shape[1]), BF16) for a in a_list] if stacked else [],
    )
    return pl.pallas_call(
        functools.partial(_down_kernel, n_a=len(a_list), scale=scale, stacked=stacked),
        grid_spec=grid_spec,
        out_shape=[jax.ShapeDtypeStruct((m, n), F32),
                   jax.ShapeDtypeStruct((m, n), BF16),
                   jax.ShapeDtypeStruct((m, LANES), F32),
                   jax.ShapeDtypeStruct((ms, n), F32),
                   jax.ShapeDtypeStruct((ms, n), BF16),
                   jax.ShapeDtypeStruct((ms, LANES), F32)],
        compiler_params=_cparams(2),
        name=name,
    )(lidx, *a_list, *as_list, *([w] * len(a_list)), res, ress, g_next)


HIST_A = 16
HIST_B = 32
HIST_C = 8


def _mix_abc_kernel(l_ref, *refs, tt, tr, t_valid, n_prev, n_chunks, has_state, dg):
    if has_state:
        (z_ref, sp_ref, sb_ref, sc_ref, wpool_ref, pscale_ref, wdwb_ref, bdwb_ref, lng_ref, lnb_ref, wdwc_ref,
         y_ref, np_ref, nb_ref, nc_ref, full_a, full_b, full_c, shift_b, pooled_s) = refs
    else:
        (z_ref, wpool_ref, pscale_ref, wdwb_ref, bdwb_ref, lng_ref, lnb_ref, wdwc_ref,
         y_ref, np_ref, nb_ref, nc_ref, full_a, full_b, full_c, shift_b, pooled_s) = refs
    c = pl.program_id(1)

    @pl.when(c == 0)
    def _():
        full_a[0:HIST_A, :] = jnp.zeros((HIST_A, dg), F32)
        full_b[0:HIST_B, :] = jnp.zeros((HIST_B, dg), F32)
        full_c[0:HIST_C, :] = jnp.zeros((HIST_C, dg), F32)
        if has_state:
            full_a[HIST_A - POOL_BUF:HIST_A, :] = sp_ref[...]
            full_b[HIST_B - (CONV_B_WIDTH - 1):HIST_B, :] = sb_ref[...]
            full_c[HIST_C - (CONV_C_WIDTH - 1):HIST_C, :] = sc_ref[...]

    full_a[HIST_A:HIST_A + tt, :] = z_ref[:, 0:dg]
    full_b[HIST_B:HIST_B + tt, :] = z_ref[:, dg:2 * dg] * jax.nn.sigmoid(z_ref[:, 2 * dg:3 * dg])
    full_c[HIST_C:HIST_C + tt, :] = z_ref[:, 4 * dg:5 * dg] * z_ref[:, 5 * dg:6 * dg]

    n_shift = HIST_B + tt - SUBLANES
    for s in range(1, SUBLANES):
        shift_b[s - 1, :, :] = full_b[s:s + n_shift, :]

    def tap(w_ref, j):
        w = w_ref[j]
        return w if tr == SUBLANES else jnp.concatenate([w] * (tr // SUBLANES), axis=0)

    pch = dg // len(POOL_WINDOWS)
    for r0 in range(0, tt, tr):
        t_glob = c * tt + r0 + lax.broadcasted_iota(jnp.int32, (tr, 1), 0)
        for g, w in enumerate(POOL_WINDOWS):
            cols = slice(g * pch, (g + 1) * pch)
            cur = full_a[HIST_A + r0:HIST_A + r0 + tr, cols]
            acc = cur
            for s in range(1, w):
                acc = acc + full_a[HIST_A + r0 - s:HIST_A + r0 - s + tr, cols]
            cnt = jnp.minimum(w, n_prev + t_glob + 1).astype(F32)
            pooled_s[r0:r0 + tr, cols] = acc / cnt - cur

        acc = jnp.zeros((tr, dg), F32) + bdwb_ref[...]
        for j in range(CONV_B_WIDTH):
            off = HIST_B - (CONV_B_WIDTH - 1) + j
            base, s = r0 + off - off % SUBLANES, off % SUBLANES
            src = full_b[base:base + tr, :] if s == 0 else shift_b[s - 1, base:base + tr, :]
            acc = acc + tap(wdwb_ref, j) * src
        mu = jnp.mean(acc, axis=-1, keepdims=True)
        xc = acc - mu
        var = jnp.mean(xc * xc, axis=-1, keepdims=True)
        yn = xc * lax.rsqrt(var + EPS) * lng_ref[...] + lnb_ref[...]
        y_ref[r0:r0 + tr, dg:2 * dg] = jax.nn.silu(yn).astype(y_ref.dtype)

        acc = jnp.zeros((tr, dg), F32)
        off = HIST_C - (CONV_C_WIDTH - 1) + r0
        for j in range(CONV_C_WIDTH):
            acc = acc + tap(wdwc_ref, j) * full_c[off + j:off + j + tr, :]
        y_ref[r0:r0 + tr, 2 * dg:3 * dg] = (z_ref[r0:r0 + tr, 3 * dg:4 * dg] * acc).astype(y_ref.dtype)

    for g in range(len(POOL_WINDOWS)):
        cols = slice(g * pch, (g + 1) * pch)
        ya = jnp.dot(pooled_s[:, cols].astype(BF16), wpool_ref[g].astype(BF16), preferred_element_type=F32)
        y_ref[:, cols] = (ya * pscale_ref[:, cols]).astype(y_ref.dtype)

    @pl.when(c == n_chunks - 1)
    def _():
        np_ref[...] = full_a[HIST_A + t_valid - POOL_BUF:HIST_A + t_valid, :]
        nb_ref[...] = full_b[HIST_B + t_valid - (CONV_B_WIDTH - 1):HIST_B + t_valid, :]
        nc_ref[...] = full_c[HIST_C + t_valid - (CONV_C_WIDTH - 1):HIST_C + t_valid, :]

    if n_chunks > 1:
        @pl.when(c < n_chunks - 1)
        def _():
            full_a[0:HIST_A, :] = full_a[tt:tt + HIST_A, :]
            full_b[0:HIST_B, :] = full_b[tt:tt + HIST_B, :]
            full_c[0:HIST_C, :] = full_c[tt:tt + HIST_C, :]


def _mix_abc(lidx, z, states, wts, *, tt, tr, t_valid, n_prev, out_dtype, name):
    b, t, _ = z.shape
    w_pool, pool_scale, w_dw_b, b_dw_b, ln_g, ln_b, w_dw_c = wts
    dg = pool_scale.shape[-1]
    n_chunks = t // tt
    has_state = states is not None

    def lsel(*tail):
        return lambda bi, ci, l: (l[0],) + tail

    in_specs = [pl.BlockSpec((None, tt, 6 * dg), lambda bi, ci, l: (bi, ci, 0))]
    args = [z]
    if has_state:
        for s in states:
            in_specs.append(pl.BlockSpec((None, None) + s.shape[2:], lambda bi, ci, l: (l[0], bi, 0, 0)))
            args.append(s)
    in_specs += [
        pl.BlockSpec((None,) + w_pool.shape[1:], lsel(0, 0, 0)),
        pl.BlockSpec((None, 1, dg), lsel(0, 0)),
        pl.BlockSpec((None, CONV_B_WIDTH, SUBLANES, dg), lsel(0, 0, 0)),
        pl.BlockSpec((None, 1, dg), lsel(0, 0)),
        pl.BlockSpec((None, 1, dg), lsel(0, 0)),
        pl.BlockSpec((None, 1, dg), lsel(0, 0)),
        pl.BlockSpec((None, CONV_C_WIDTH, SUBLANES, dg), lsel(0, 0, 0)),
    ]
    args += [w_pool, pool_scale, w_dw_b, b_dw_b, ln_g, ln_b, w_dw_c]
    grid_spec = pltpu.PrefetchScalarGridSpec(
        num_scalar_prefetch=1,
        grid=(b, n_chunks),
        in_specs=in_specs,
        out_specs=[pl.BlockSpec((None, tt, 3 * dg), lambda bi, ci, l: (bi, ci, 0)),
                   pl.BlockSpec((None, POOL_BUF, dg), lambda bi, ci, l: (bi, 0, 0)),
                   pl.BlockSpec((None, CONV_B_WIDTH - 1, dg), lambda bi, ci, l: (bi, 0, 0)),
                   pl.BlockSpec((None, CONV_C_WIDTH - 1, dg), lambda bi, ci, l: (bi, 0, 0))],
        scratch_shapes=[pltpu.VMEM((HIST_A + tt, dg), F32),
                        pltpu.VMEM((HIST_B + tt, dg), F32),
                        pltpu.VMEM((HIST_C + tt, dg), F32),
                        pltpu.VMEM((SUBLANES - 1, HIST_B + tt - SUBLANES, dg), F32),
                        pltpu.VMEM((tt, dg), F32)],
    )
    return pl.pallas_call(
        functools.partial(_mix_abc_kernel, tt=tt, tr=tr, t_valid=t_valid, n_prev=n_prev,
                          n_chunks=n_chunks, has_state=has_state, dg=dg),
        grid_spec=grid_spec,
        out_shape=[jax.ShapeDtypeStruct((b, t, 3 * dg), out_dtype),
                   jax.ShapeDtypeStruct((b, POOL_BUF, dg), F32),
                   jax.ShapeDtypeStruct((b, CONV_B_WIDTH - 1, dg), F32),
                   jax.ShapeDtypeStruct((b, CONV_C_WIDTH - 1, dg), F32)],
        compiler_params=_cparams(2),
        name=name,
    )(lidx, *args)


def _head_rmsnorm(x, g):
    return x * lax.rsqrt(jnp.mean(x * x, axis=-1, keepdims=True) + EPS) * g


def _dot_nt(a, b):
    return lax.dot_general(a, b, (((1,), (1,)), ((), ())), preferred_element_type=F32)


def _attn_prompt_kernel(l_ref, q_ref, k_ref, v_ref, gq_ref, gk_ref, y_ref, ko_ref, vo_ref,
                        qn_s, m_s, l_s, acc_s, qf_s, kf_s, vf_s, mf_s, lf_s, accf_s):
    t = q_ref.shape[0]
    blk = ATT_BLK
    (w0, d0), (w1, d1), (w2, d2) = DILATED_CFG
    assert d0 == 1 and d2 % d1 == 0 and w0 // d0 == blk and w1 // d1 == blk and w2 // d2 == blk
    assert (t // d2) % blk == 0
    fold, inner, tf = d1, d2 // d1, t // d1

    qn_s[...] = _head_rmsnorm(q_ref[...], gq_ref[...]) * (HEAD_DIM ** -0.5)
    ko_ref[...] = _head_rmsnorm(k_ref[...], gk_ref[...])
    vo_ref[...] = v_ref[...]

    row1 = lax.broadcasted_iota(jnp.int32, (blk, blk), 0)
    col1 = lax.broadcasted_iota(jnp.int32, (blk, blk), 1)
    causal_mask = col1 <= row1
    row2 = lax.broadcasted_iota(jnp.int32, (blk, 2 * blk), 0)
    col2 = lax.broadcasted_iota(jnp.int32, (blk, 2 * blk), 1)
    band_mask = (col2 >= row2) & (col2 <= row2 + blk)

    def rows(start, n, stride):
        return pl.ds(start, n, stride=stride) if stride > 1 else pl.ds(start, n)

    def update(qkv, state, start, stride, with_prev, first_branch):
        q_r, k_r, v_r = qkv
        m_r, l_r, acc_r = state
        qidx = rows(start, blk, stride)
        kidx = rows(start - blk * stride, 2 * blk, stride) if with_prev else qidx
        qb = q_r[qidx, :].astype(BF16)
        s = _dot_nt(qb, k_r[kidx, :].astype(BF16))
        s = jnp.where(band_mask if with_prev else causal_mask, s, NEG)
        mx = jnp.max(s, axis=-1, keepdims=True)
        vb = v_r[kidx, :].astype(BF16)
        if first_branch:
            p = jnp.exp(s - mx)
            m_r[qidx, :] = jnp.broadcast_to(mx, (blk, LANES))
            l_r[qidx, :] = jnp.broadcast_to(jnp.sum(p, axis=-1, keepdims=True), (blk, LANES))
            acc_r[qidx, :] = jnp.dot(p.astype(BF16), vb, preferred_element_type=F32)
        else:
            m_old = m_r[qidx, :]
            m_new = jnp.maximum(m_old, mx)
            alpha = jnp.exp(m_old - m_new)
            p = jnp.exp(s - (jnp.concatenate([m_new, m_new], axis=-1) if with_prev else m_new))
            m_r[qidx, :] = m_new
            l_r[qidx, :] = alpha * l_r[qidx, :] + jnp.sum(p, axis=-1, keepdims=True)
            acc_r[qidx, :] = alpha * acc_r[qidx, :] + jnp.dot(p.astype(BF16), vb, preferred_element_type=F32)

    natural = ((qn_s, ko_ref, v_ref), (m_s, l_s, acc_s))
    folded = ((qf_s, kf_s, vf_s), (mf_s, lf_s, accf_s))

    for i in range(t // blk):
        update(*natural, i * blk, 1, i > 0, True)

    for src, dst in zip(natural[0] + natural[1], folded[0] + folded[1]):
        for r in range(fold):
            dst[r * tf:(r + 1) * tf, :] = src[pl.ds(r, tf, stride=fold), :]

    for r in range(fold):
        for i in range(tf // blk):
            update(*folded, r * tf + i * blk, 1, i > 0, False)

    for r in range(fold):
        for r2 in range(inner):
            for i in range(tf // inner // blk):
                update(*folded, r * tf + r2 + i * blk * inner, inner, i > 0, False)

    accf_s[...] = accf_s[...] / lf_s[...]
    for r in range(fold):
        acc_s[pl.ds(r, tf, stride=fold), :] = accf_s[r * tf:(r + 1) * tf, :]
    y_ref[...] = acc_s[...].astype(y_ref.dtype)


def _attn_prompt(lidx, z, gq, gk, *, n_heads):
    b, t, n_in = z.shape
    dd = n_heads * HEAD_DIM
    q0 = (n_in - 3 * dd) // HEAD_DIM
    hd = HEAD_DIM
    head_out = pl.BlockSpec((None, t, hd), lambda bi, h, l: (bi, 0, h))
    grid_spec = pltpu.PrefetchScalarGridSpec(
        num_scalar_prefetch=1,
        grid=(b, n_heads),
        in_specs=[pl.BlockSpec((None, t, hd), lambda bi, h, l: (bi, 0, q0 + h)),
                  pl.BlockSpec((None, t, hd), lambda bi, h, l: (bi, 0, q0 + n_heads + h)),
                  pl.BlockSpec((None, t, hd), lambda bi, h, l: (bi, 0, q0 + 2 * n_heads + h)),
                  pl.BlockSpec((None, 1, hd), lambda bi, h, l: (l[0], 0, 0)),
                  pl.BlockSpec((None, 1, hd), lambda bi, h, l: (l[0], 0, 0))],
        out_specs=[head_out, head_out, head_out],
        scratch_shapes=[pltpu.VMEM((t, hd), F32)] * 10,
    )
    return pl.pallas_call(
        _attn_prompt_kernel,
        grid_spec=grid_spec,
        out_shape=[jax.ShapeDtypeStruct((b, t, dd), BF16),
                   jax.ShapeDtypeStruct((b, t, dd), F32),
                   jax.ShapeDtypeStruct((b, t, dd), F32)],
        compiler_params=_cparams(2),
        name="attn_prompt",
    )(lidx, z, z, z, gq, gk)


def _attn_sample_kernel(l_ref, qkv_ref, kc_ref, vc_ref, gq_ref, gk_ref, y_ref, ko_ref, *, t_valid, n_heads):
    tq = qkv_ref.shape[0]
    hd = HEAD_DIM
    lc = kc_ref.shape[0] // n_heads

    def multiplicity(dist):
        w = jnp.zeros(dist.shape, F32)
        for window, dil in DILATED_CFG:
            hit = (dist >= 0) & (dist <= window) & ((dist & (dil - 1)) == 0)
            w = w + hit.astype(F32)
        return w

    d1 = lc + lax.broadcasted_iota(jnp.int32, (tq, lc), 0) - lax.broadcasted_iota(jnp.int32, (tq, lc), 1)
    w1 = multiplicity(d1)
    d2 = lax.broadcasted_iota(jnp.int32, (tq, tq), 0) - lax.broadcasted_iota(jnp.int32, (tq, tq), 1)
    w2 = multiplicity(d2) * (lax.broadcasted_iota(jnp.int32, (tq, tq), 1) < t_valid).astype(F32)

    for h in range(n_heads):
        q = qkv_ref[:, h * hd:(h + 1) * hd]
        k = qkv_ref[:, (n_heads + h) * hd:(n_heads + h + 1) * hd]
        v = qkv_ref[:, (2 * n_heads + h) * hd:(2 * n_heads + h + 1) * hd]
        qn = (_head_rmsnorm(q, gq_ref[...]) * (HEAD_DIM ** -0.5)).astype(BF16)
        kn = _head_rmsnorm(k, gk_ref[...])
        ko_ref[:, h * hd:(h + 1) * hd] = kn
        head_rows = pl.ds(h, lc, stride=n_heads)
        s1 = jnp.where(w1 > 0, _dot_nt(qn, kc_ref[head_rows, :].astype(BF16)), NEG)
        s2 = jnp.where(w2 > 0, _dot_nt(qn, kn.astype(BF16)), NEG)
        m = jnp.maximum(jnp.max(s1, axis=-1, keepdims=True), jnp.max(s2, axis=-1, keepdims=True))
        p1 = w1 * jnp.exp(s1 - m)
        p2 = w2 * jnp.exp(s2 - m)
        den = jnp.sum(p1, axis=-1, keepdims=True) + jnp.sum(p2, axis=-1, keepdims=True)
        num = (jnp.dot(p1.astype(BF16), vc_ref[head_rows, :].astype(BF16), preferred_element_type=F32)
               + jnp.dot(p2.astype(BF16), v.astype(BF16), preferred_element_type=F32))
        y_ref[:, h * hd:(h + 1) * hd] = (num / den).astype(y_ref.dtype)


def _attn_sample(lidx, z, cache_k, cache_v, gq, gk, *, n_heads, t_valid):
    b, tq, n_in = z.shape
    depth, _, lc = cache_k.shape[:3]
    hd = HEAD_DIM
    dd = n_heads * hd
    assert (n_in - 3 * dd) % (3 * dd) == 0
    qkv_block = (n_in - 3 * dd) // (3 * dd)
    for window, dil in DILATED_CFG:
        assert dil & (dil - 1) == 0 and window <= lc
    ck = cache_k.reshape(depth, b, lc * n_heads, hd)
    cv = cache_v.reshape(depth, b, lc * n_heads, hd)
    grid_spec = pltpu.PrefetchScalarGridSpec(
        num_scalar_prefetch=1,
        grid=(b,),
        in_specs=[pl.BlockSpec((None, tq, 3 * dd), lambda bi, l: (bi, 0, qkv_block)),
                  pl.BlockSpec((None, None, lc * n_heads, hd), lambda bi, l: (l[0], bi, 0, 0)),
                  pl.BlockSpec((None, None, lc * n_heads, hd), lambda bi, l: (l[0], bi, 0, 0)),
                  pl.BlockSpec((None, 1, hd), lambda bi, l: (l[0], 0, 0)),
                  pl.BlockSpec((None, 1, hd), lambda bi, l: (l[0], 0, 0))],
        out_specs=[pl.BlockSpec((None, tq, dd), lambda bi, l: (bi, 0, 0)),
                   pl.BlockSpec((None, tq, dd), lambda bi, l: (bi, 0, 0))],
    )
    return pl.pallas_call(
        functools.partial(_attn_sample_kernel, t_valid=t_valid, n_heads=n_heads),
        grid_spec=grid_spec,
        out_shape=[jax.ShapeDtypeStruct((b, tq, dd), F32),
                   jax.ShapeDtypeStruct((b, tq, dd), F32)],
        compiler_params=_cparams(1),
        name="attn_sample",
    )(lidx, z, ck, cv, gq, gk)


SAMPLE_ROWS = 8
TM = 1024
TN_UP = 256
TN_WIDE = 512


def kernel(x_prompt, x_sample, state_pool, state_conv_b, state_conv_c, cache_k, cache_v, g_ffn1, w_ffn1_gate, w_ffn1_up, w_ffn1_down, g_mix, w_in, w_pool, pool_scale, w_dw_b, b_dw_b, ln_b_g, ln_b_b, w_dw_c, q_norm_g, k_norm_g, w_out, g_ffn2, w_ffn2_gate, w_ffn2_up, w_ffn2_down):
    bp, tp, d = x_prompt.shape
    bs, ts, _ = x_sample.shape
    depth = g_ffn1.shape[0]
    dg = pool_scale.shape[-1]
    n_heads = cache_k.shape[3]
    n_in = w_in.shape[-1]
    assert ts <= SAMPLE_ROWS and state_pool.shape[2] == POOL_BUF

    row = lambda a: a.reshape(depth, 1, a.shape[-1])
    g1, gm, g2 = row(g_ffn1), row(g_mix), row(g_ffn2)
    taps = lambda a: jnp.broadcast_to(a[:, :, None, :], a.shape[:2] + (SUBLANES, a.shape[-1]))
    mix_w = (w_pool, row(pool_scale), taps(w_dw_b), row(b_dw_b), row(ln_b_g), row(ln_b_b), taps(w_dw_c))
    gq, gk = row(q_norm_g), row(k_norm_g)

    xp = x_prompt.reshape(bp * tp, d)
    xs = x_sample.reshape(bs * ts, d)
    xgp, ssqp = _prep(xp, g1[0], 256)
    xgs, ssqs = _prep(xs, g1[0], bs * ts)

    def ffn(lidx, st, wg, wu, wd, g_next, tag):
        xp, xgp, ssqp, xs, xgs, ssqs = st
        ap, a_s = _normed_matmul(lidx, xgp, ssqp, xgs, ssqs, (wg, wu), tm=TM, tn=TN_UP, out_dtype=BF16,
                                 name="ffn_up" + tag)
        return tuple(_residual_matmul(lidx, [ap], [a_s], wd, [0], xp, xs, g_next, scale=0.5, tm=TM, tn=TN_UP,
                                      stacked=False, name="ffn_down" + tag))

    def layer_step(st, l):
        lnext = jnp.minimum(l + 1, depth - 1)
        l_same = jnp.stack([l, l]).astype(jnp.int32)
        l_next = jnp.stack([l, lnext]).astype(jnp.int32)

        st = ffn(l_same, st, w_ffn1_gate, w_ffn1_up, w_ffn1_down, gm, "1")
        xp, xgp, ssqp, xs, xgs, ssqs = st
        zp, zs = _normed_matmul(l_same, xgp, ssqp, xgs, ssqs, (w_in,), tm=TM, tn=TN_WIDE, out_dtype=F32,
                                name="proj_in")

        zp = zp.reshape(bp, tp, n_in)
        yabc, pool_p, convb_p, convc_p = _mix_abc(
            l_same, zp, None, mix_w, tt=128, tr=16, t_valid=128, n_prev=0, out_dtype=BF16, name="mix_abc_p")
        yd, k_p, v_p = _attn_prompt(l_same, zp, gq, gk, n_heads=n_heads)

        zs = jnp.pad(zs.reshape(bs, ts, n_in), ((0, 0), (0, SAMPLE_ROWS - ts), (0, 0)))
        yabc_s, pool_s, convb_s, convc_s = _mix_abc(
            l_same, zs, (state_pool, state_conv_b, state_conv_c), mix_w, tt=SAMPLE_ROWS, tr=SAMPLE_ROWS,
            t_valid=ts, n_prev=POOL_BUF, out_dtype=F32, name="mix_abc_s")
        yd_s, k_s = _attn_sample(l_same, zs, cache_k, cache_v, gq, gk, n_heads=n_heads, t_valid=ts)
        v_s = zs[:, :ts, n_in - dg:]

        a_p = [yabc.reshape(bp * tp, 3 * dg), yd.reshape(bp * tp, dg)]
        a_s = [yabc_s[:, :ts].reshape(bs * ts, 3 * dg).astype(BF16), yd_s[:, :ts].reshape(bs * ts, dg).astype(BF16)]
        st = tuple(_residual_matmul(l_same, a_p, a_s, w_out, [0, 3], xp, xs, g2, scale=1.0, tm=TM, tn=TN_WIDE,
                                    stacked=True, name="proj_out"))
        st = ffn(l_next, st, w_ffn2_gate, w_ffn2_up, w_ffn2_down, g1, "2")

        outs = (pool_p, pool_s, convb_p, convb_s, convc_p, convc_s,
                k_p.reshape(bp, tp, n_heads, HEAD_DIM), v_p.reshape(bp, tp, n_heads, HEAD_DIM),
                k_s[:, :ts].reshape(bs, ts, n_heads, HEAD_DIM), v_s.reshape(bs, ts, n_heads, HEAD_DIM))
        return st, outs

    st, outs = lax.scan(layer_step, (xp, xgp, ssqp, xs, xgs, ssqs), jnp.arange(depth, dtype=jnp.int32))
    return (st[0].reshape(bp, tp, d), st[3].reshape(bs, ts, d)) + tuple(outs)
```

```python
import functools

import jax
import jax.numpy as jnp
from jax import lax
from jax.experimental import pallas as pl
from jax.experimental.pallas import tpu as pltpu

EPS = 1e-6
NEG = -1e30
LANES = 128
SUBLANES = 8
HEAD_DIM = 128
POOL_WINDOWS = (2, 4, 8, 16)
POOL_BUF = max(POOL_WINDOWS) - 1
CONV_B_WIDTH = 31
CONV_C_WIDTH = 3
DILATED_CFG = ((128, 1), (512, 4), (2048, 16))
ATT_BLK = 128
VMEM_LIMIT = 60 * 1024 * 1024

F32 = jnp.float32
BF16 = jnp.bfloat16


def _cparams(n_axes):
    return pltpu.CompilerParams(dimension_semantics=("arbitrary",) * n_axes,
                                vmem_limit_bytes=VMEM_LIMIT)


def _lane_fold(v):
    n = v.shape[-1] // LANES
    out = v[:, 0:LANES]
    for k in range(1, n):
        out = out + v[:, k * LANES:(k + 1) * LANES]
    return out


def _row_rms_scale(ssq, d_model):
    return lax.rsqrt(jnp.sum(ssq, axis=-1, keepdims=True) * (1.0 / d_model) + EPS)


def _prep_kernel(x_ref, g_ref, x_out, xg_ref, ssq_ref):
    x = x_ref[...]
    x_out[...] = x
    xg_ref[...] = (x * g_ref[...]).astype(BF16)
    ssq_ref[...] = _lane_fold(x * x)


def _prep(x, g_row, tm):
    m, d = x.shape
    return pl.pallas_call(
        _prep_kernel,
        grid=(m // tm,),
        in_specs=[pl.BlockSpec((tm, d), lambda i: (i, 0)),
                  pl.BlockSpec((1, d), lambda i: (0, 0))],
        out_specs=[pl.BlockSpec((tm, d), lambda i: (i, 0)),
                   pl.BlockSpec((tm, d), lambda i: (i, 0)),
                   pl.BlockSpec((tm, LANES), lambda i: (i, 0))],
        out_shape=[jax.ShapeDtypeStruct((m, d), F32),
                   jax.ShapeDtypeStruct((m, d), BF16),
                   jax.ShapeDtypeStruct((m, LANES), F32)],
        compiler_params=_cparams(1),
        name="norm_prep",
    )(x, g_row)


def _side_col(nj):
    return lambda i, j: jnp.where(i == 0, j, nj - 1)


def _up_kernel(l_ref, xg_ref, ssq_ref, xgs_ref, ssqs_ref, *rest, n_w, d_model):
    w_refs, (o_ref, os_ref, lhs_s) = rest[:n_w], rest[n_w:]
    tm = xg_ref.shape[0]
    i, j = pl.program_id(0), pl.program_id(1)

    def weights():
        return [w[...].astype(BF16) for w in w_refs]

    def act(outs):
        return outs[0] if n_w == 1 else jax.nn.silu(outs[0]) * outs[1]

    @pl.when((i == 0) & (j == 0))
    def _():
        lhs_s[0:tm, :] = xg_ref[...]
        lhs_s[tm:, :] = xgs_ref[...]

    @pl.when(i == 0)
    def _():
        r = _row_rms_scale(ssq_ref[...], d_model)
        rs = _row_rms_scale(ssqs_ref[...], d_model)
        lhs = lhs_s[...]
        outs = [jnp.dot(lhs, w, preferred_element_type=F32) for w in weights()]
        o_ref[...] = act([o[0:tm] * r for o in outs]).astype(o_ref.dtype)
        os_ref[...] = act([o[tm:] * rs for o in outs]).astype(os_ref.dtype)

    @pl.when(i > 0)
    def _():
        r = _row_rms_scale(ssq_ref[...], d_model)
        xg = xg_ref[...]
        o_ref[...] = act([jnp.dot(xg, w, preferred_element_type=F32) * r for w in weights()]).astype(o_ref.dtype)


def _normed_matmul(lidx, xg, ssq, xgs, ssqs, ws, *, tm, tn, out_dtype, name):
    m, k = xg.shape
    ms = xgs.shape[0]
    n = ws[0].shape[-1]
    sj = _side_col(n // tn)
    grid_spec = pltpu.PrefetchScalarGridSpec(
        num_scalar_prefetch=1,
        grid=(m // tm, n // tn),
        in_specs=[pl.BlockSpec((tm, k), lambda i, j, l: (i, 0)),
                  pl.BlockSpec((tm, LANES), lambda i, j, l: (i, 0)),
                  pl.BlockSpec((ms, k), lambda i, j, l: (0, 0)),
                  pl.BlockSpec((ms, LANES), lambda i, j, l: (0, 0))]
                 + [pl.BlockSpec((None, k, tn), lambda i, j, l: (l[0], 0, j)) for _ in ws],
        out_specs=[pl.BlockSpec((tm, tn), lambda i, j, l: (i, j)),
                   pl.BlockSpec((ms, tn), lambda i, j, l: (0, sj(i, j)))],
        scratch_shapes=[pltpu.VMEM((tm + ms, k), BF16)],
    )
    return pl.pallas_call(
        functools.partial(_up_kernel, n_w=len(ws), d_model=k),
        grid_spec=grid_spec,
        out_shape=[jax.ShapeDtypeStruct((m, n), out_dtype),
                   jax.ShapeDtypeStruct((ms, n), out_dtype)],
        compiler_params=_cparams(2),
        name=name,
    )(lidx, xg, ssq, xgs, ssqs, *ws)


def _down_kernel(l_ref, *refs, n_a, scale, stacked):
    a_refs, as_refs, w_refs = refs[:n_a], refs[n_a:2 * n_a], refs[2 * n_a:3 * n_a]
    res_ref, ress_ref, g_ref, x_out, xg_out, ssq_out, xs_out, xgs_out, ssqs_out = refs[3 * n_a:3 * n_a + 9]
    lhs_refs = refs[3 * n_a + 9:]
    tm = res_ref.shape[0]
    i, j = pl.program_id(0), pl.program_id(1)
    first_col = j == 0

    def weights():
        return [w[...].astype(BF16) for w in w_refs]

    def matmul(lhs_group, wb):
        acc = None
        for a, w in zip(lhs_group, wb):
            part = jnp.dot(a[...], w, preferred_element_type=F32)
            acc = part if acc is None else acc + part
        return acc

    def finish(acc, res, x_o, xg_o, ssq_o):
        xn = res[...] + scale * acc
        x_o[...] = xn
        xg_o[...] = (xn * g_ref[...]).astype(BF16)
        part_ssq = _lane_fold(xn * xn)

        @pl.when(first_col)
        def _():
            ssq_o[...] = part_ssq

        @pl.when(jnp.logical_not(first_col))
        def _():
            ssq_o[...] += part_ssq

    if not stacked:
        wb = weights()
        finish(matmul(a_refs, wb), res_ref, x_out, xg_out, ssq_out)

        @pl.when(i == 0)
        def _():
            finish(matmul(as_refs, wb), ress_ref, xs_out, xgs_out, ssqs_out)
        return

    @pl.when((i == 0) & first_col)
    def _():
        for lhs, a, a_s in zip(lhs_refs, a_refs, as_refs):
            lhs[0:tm, :] = a[...]
            lhs[tm:, :] = a_s[...]

    @pl.when(i == 0)
    def _():
        acc = matmul(lhs_refs, weights())
        finish(acc[0:tm], res_ref, x_out, xg_out, ssq_out)
        finish(acc[tm:], ress_ref, xs_out, xgs_out, ssqs_out)

    @pl.when(i > 0)
    def _():
        finish(matmul(a_refs, weights()), res_ref, x_out, xg_out, ssq_out)


def _residual_matmul(lidx, a_list, as_list, w, row_blocks, res, ress, g_next, *, scale, tm, tn, stacked, name):
    m, ms = res.shape[0], ress.shape[0]
    n = w.shape[-1]
    sj = _side_col(n // tn)
    in_specs = [pl.BlockSpec((tm, a.shape[1]), lambda i, j, l: (i, 0), pipeline_mode=pl.Buffered(1))
                for a in a_list]
    in_specs += [pl.BlockSpec((ms, a.shape[1]), lambda i, j, l: (0, 0)) for a in as_list]
    in_specs += [pl.BlockSpec((None, a.shape[1], tn), functools.partial(lambda i, j, l, rb: (l[0], rb, j), rb=rb))
                 for a, rb in zip(a_list, row_blocks)]
    in_specs += [pl.BlockSpec((tm, tn), lambda i, j, l: (i, j)),
                 pl.BlockSpec((ms, tn), lambda i, j, l: (0, sj(i, j))),
                 pl.BlockSpec((None, 1, tn), lambda i, j, l: (l[1], 0, j))]
    grid_spec = pltpu.PrefetchScalarGridSpec(
        num_scalar_prefetch=1,
        grid=(m // tm, n // tn),
        in_specs=in_specs,
        out_specs=[pl.BlockSpec((tm, tn), lambda i, j, l: (i, j)),
                   pl.BlockSpec((tm, tn), lambda i, j, l: (i, j)),
                   pl.BlockSpec((tm, LANES), lambda i, j, l: (i, 0)),
                   pl.BlockSpec((ms, tn), lambda i, j, l: (0, sj(i, j))),
                   pl.BlockSpec((ms, tn), lambda i, j, l: (0, sj(i, j))),
                   pl.BlockSpec((ms, LANES), lambda i, j, l: (0, 0))],
        scratch_shapes=[pltpu.VMEM((tm + ms, a.shape[1]), BF16) for a in a_list] if stacked else [],
    )
    return pl.pallas_call(
        functools.partial(_down_kernel, n_a=len(a_list), scale=scale, stacked=stacked),
        grid_spec=grid_spec,
        out_shape=[jax.ShapeDtypeStruct((m, n), F32),
                   jax.ShapeDtypeStruct((m, n), BF16),
                   jax.ShapeDtypeStruct((m, LANES), F32),
                   jax.ShapeDtypeStruct((ms, n), F32),
                   jax.ShapeDtypeStruct((ms, n), BF16),
                   jax.ShapeDtypeStruct((ms, LANES), F32)],
        compiler_params=_cparams(2),
        name=name,
    )(lidx, *a_list, *as_list, *([w] * len(a_list)), res, ress, g_next)


HIST_A = 16
HIST_B = 32
HIST_C = 8


def _mix_abc_kernel(l_ref, *refs, tt, tr, t_valid, n_prev, n_chunks, has_state, dg):
    if has_state:
        (z_ref, sp_ref, sb_ref, sc_ref, wpool_ref, pscale_ref, wdwb_ref, bdwb_ref, lng_ref, lnb_ref, wdwc_ref,
         y_ref, np_ref, nb_ref, nc_ref, full_a, full_b, full_c, shift_b, pooled_s) = refs
    else:
        (z_ref, wpool_ref, pscale_ref, wdwb_ref, bdwb_ref, lng_ref, lnb_ref, wdwc_ref,
         y_ref, np_ref, nb_ref, nc_ref, full_a, full_b, full_c, shift_b, pooled_s) = refs
    c = pl.program_id(1)

    @pl.when(c == 0)
    def _():
        full_a[0:HIST_A, :] = jnp.zeros((HIST_A, dg), F32)
        full_b[0:HIST_B, :] = jnp.zeros((HIST_B, dg), F32)
        full_c[0:HIST_C, :] = jnp.zeros((HIST_C, dg), F32)
        if has_state:
            full_a[HIST_A - POOL_BUF:HIST_A, :] = sp_ref[...]
            full_b[HIST_B - (CONV_B_WIDTH - 1):HIST_B, :] = sb_ref[...]
            full_c[HIST_C - (CONV_C_WIDTH - 1):HIST_C, :] = sc_ref[...]

    full_a[HIST_A:HIST_A + tt, :] = z_ref[:, 0:dg]
    full_b[HIST_B:HIST_B + tt, :] = z_ref[:, dg:2 * dg] * jax.nn.sigmoid(z_ref[:, 2 * dg:3 * dg])
    full_c[HIST_C:HIST_C + tt, :] = z_ref[:, 4 * dg:5 * dg] * z_ref[:, 5 * dg:6 * dg]

    n_shift = HIST_B + tt - SUBLANES
    for s in range(1, SUBLANES):
        shift_b[s - 1, :, :] = full_b[s:s + n_shift, :]

    def tap(w_ref, j):
        w = w_ref[j]
        return w if tr == SUBLANES else jnp.concatenate([w] * (tr // SUBLANES), axis=0)

    pch = dg // len(POOL_WINDOWS)
    for r0 in range(0, tt, tr):
        t_glob = c * tt + r0 + lax.broadcasted_iota(jnp.int32, (tr, 1), 0)
        for g, w in enumerate(POOL_WINDOWS):
            cols = slice(g * pch, (g + 1) * pch)
            cur = full_a[HIST_A + r0:HIST_A + r0 + tr, cols]
            acc = cur
            for s in range(1, w):
                acc = acc + full_a[HIST_A + r0 - s:HIST_A + r0 - s + tr, cols]
            cnt = jnp.minimum(w, n_prev + t_glob + 1).astype(F32)
            pooled_s[r0:r0 + tr, cols] = acc / cnt - cur

        acc = jnp.zeros((tr, dg), F32) + bdwb_ref[...]
        for j in range(CONV_B_WIDTH):
            off = HIST_B - (CONV_B_WIDTH - 1) + j
            base, s = r0 + off - off % SUBLANES, off % SUBLANES
            src = full_b[base:base + tr, :] if s == 0 else shift_b[s - 1, base:base + tr, :]
            acc = acc + tap(wdwb_ref, j) * src
        mu = jnp.mean(acc, axis=-1, keepdims=True)
        xc = acc - mu
        var = jnp.mean(xc * xc, axis=-1, keepdims=True)
        yn = xc * lax.rsqrt(var + EPS) * lng_ref[...] + lnb_ref[...]
        y_ref[r0:r0 + tr, dg:2 * dg] = jax.nn.silu(yn).astype(y_ref.dtype)

        acc = jnp.zeros((tr, dg), F32)
        off = HIST_C - (CONV_C_WIDTH - 1) + r0
        for j in range(CONV_C_WIDTH):
            acc = acc + tap(wdwc_ref, j) * full_c[off + j:off + j + tr, :]
        y_ref[r0:r0 + tr, 2 * dg:3 * dg] = (z_ref[r0:r0 + tr, 3 * dg:4 * dg] * acc).astype(y_ref.dtype)

    for g in range(len(POOL_WINDOWS)):
        cols = slice(g * pch, (g + 1) * pch)
        ya = jnp.dot(pooled_s[:, cols].astype(BF16), wpool_ref[g].astype(BF16), preferred_element_type=F32)
        y_ref[:, cols] = (ya * pscale_ref[:, cols]).astype(y_ref.dtype)

    @pl.when(c == n_chunks - 1)
    def _():
        np_ref[...] = full_a[HIST_A + t_valid - POOL_BUF:HIST_A + t_valid, :]
        nb_ref[...] = full_b[HIST_B + t_valid - (CONV_B_WIDTH - 1):HIST_B + t_valid, :]
        nc_ref[...] = full_c[HIST_C + t_valid - (CONV_C_WIDTH - 1):HIST_C + t_valid, :]

    if n_chunks > 1:
        @pl.when(c < n_chunks - 1)
        def _():
            full_a[0:HIST_A, :] = full_a[tt:tt + HIST_A, :]
            full_b[0:HIST_B, :] = full_b[tt:tt + HIST_B, :]
            full_c[0:HIST_C, :] = full_c[tt:tt + HIST_C, :]


def _mix_abc(lidx, z, states, wts, *, tt, tr, t_valid, n_prev, out_dtype, name):
    b, t, _ = z.shape
    w_pool, pool_scale, w_dw_b, b_dw_b, ln_g, ln_b, w_dw_c = wts
    dg = pool_scale.shape[-1]
    n_chunks = t // tt
    has_state = states is not None

    def lsel(*tail):
        return lambda bi, ci, l: (l[0],) + tail

    in_specs = [pl.BlockSpec((None, tt, 6 * dg), lambda bi, ci, l: (bi, ci, 0))]
    args = [z]
    if has_state:
        for s in states:
            in_specs.append(pl.BlockSpec((None, None) + s.shape[2:], lambda bi, ci, l: (l[0], bi, 0, 0)))
            args.append(s)
    in_specs += [
        pl.BlockSpec((None,) + w_pool.shape[1:], lsel(0, 0, 0)),
        pl.BlockSpec((None, 1, dg), lsel(0, 0)),
        pl.BlockSpec((None, CONV_B_WIDTH, SUBLANES, dg), lsel(0, 0, 0)),
        pl.BlockSpec((None, 1, dg), lsel(0, 0)),
        pl.BlockSpec((None, 1, dg), lsel(0, 0)),
        pl.BlockSpec((None, 1, dg), lsel(0, 0)),
        pl.BlockSpec((None, CONV_C_WIDTH, SUBLANES, dg), lsel(0, 0, 0)),
    ]
    args += [w_pool, pool_scale, w_dw_b, b_dw_b, ln_g, ln_b, w_dw_c]
    grid_spec = pltpu.PrefetchScalarGridSpec(
        num_scalar_prefetch=1,
        grid=(b, n_chunks),
        in_specs=in_specs,
        out_specs=[pl.BlockSpec((None, tt, 3 * dg), lambda bi, ci, l: (bi, ci, 0)),
                   pl.BlockSpec((None, POOL_BUF, dg), lambda bi, ci, l: (bi, 0, 0)),
                   pl.BlockSpec((None, CONV_B_WIDTH - 1, dg), lambda bi, ci, l: (bi, 0, 0)),
                   pl.BlockSpec((None, CONV_C_WIDTH - 1, dg), lambda bi, ci, l: (bi, 0, 0))],
        scratch_shapes=[pltpu.VMEM((HIST_A + tt, dg), F32),
                        pltpu.VMEM((HIST_B + tt, dg), F32),
                        pltpu.VMEM((HIST_C + tt, dg), F32),
                        pltpu.VMEM((SUBLANES - 1, HIST_B + tt - SUBLANES, dg), F32),
                        pltpu.VMEM((tt, dg), F32)],
    )
    return pl.pallas_call(
        functools.partial(_mix_abc_kernel, tt=tt, tr=tr, t_valid=t_valid, n_prev=n_prev,
                          n_chunks=n_chunks, has_state=has_state, dg=dg),
        grid_spec=grid_spec,
        out_shape=[jax.ShapeDtypeStruct((b, t, 3 * dg), out_dtype),
                   jax.ShapeDtypeStruct((b, POOL_BUF, dg), F32),
                   jax.ShapeDtypeStruct((b, CONV_B_WIDTH - 1, dg), F32),
                   jax.ShapeDtypeStruct((b, CONV_C_WIDTH - 1, dg), F32)],
        compiler_params=_cparams(2),
        name=name,
    )(lidx, *args)


def _head_rmsnorm(x, g):
    return x * lax.rsqrt(jnp.mean(x * x, axis=-1, keepdims=True) + EPS) * g


def _dot_nt(a, b):
    return lax.dot_general(a, b, (((1,), (1,)), ((), ())), preferred_element_type=F32)


def _attn_prompt_kernel(l_ref, q_ref, k_ref, v_ref, gq_ref, gk_ref, kstack_ref, vstack_ref, y_ref, ko_ref, vo_ref,
                        qn_s, kn_s, m_s, l_s, acc_s, qf_s, kf_s, vf_s, mf_s, lf_s, accf_s, *, n_heads):
    t = q_ref.shape[0]
    blk = ATT_BLK
    (w0, d0), (w1, d1), (w2, d2) = DILATED_CFG
    assert d0 == 1 and d2 % d1 == 0 and w0 // d0 == blk and w1 // d1 == blk and w2 // d2 == blk
    assert (t // d2) % blk == 0
    fold, inner, tf = d1, d2 // d1, t // d1

    qn_s[...] = _head_rmsnorm(q_ref[...], gq_ref[...]) * (HEAD_DIM ** -0.5)
    kn_s[...] = _head_rmsnorm(k_ref[...], gk_ref[...])
    head_rows = pl.ds(pl.program_id(1), t, stride=n_heads)
    ko_ref[head_rows, :] = kn_s[...]
    vo_ref[head_rows, :] = v_ref[...]

    row1 = lax.broadcasted_iota(jnp.int32, (blk, blk), 0)
    col1 = lax.broadcasted_iota(jnp.int32, (blk, blk), 1)
    causal_mask = col1 <= row1
    row2 = lax.broadcasted_iota(jnp.int32, (blk, 2 * blk), 0)
    col2 = lax.broadcasted_iota(jnp.int32, (blk, 2 * blk), 1)
    band_mask = (col2 >= row2) & (col2 <= row2 + blk)

    def rows(start, n, stride):
        return pl.ds(start, n, stride=stride) if stride > 1 else pl.ds(start, n)

    def update(qkv, state, start, stride, with_prev, first_branch):
        q_r, k_r, v_r = qkv
        m_r, l_r, acc_r = state
        qidx = rows(start, blk, stride)
        kidx = rows(start - blk * stride, 2 * blk, stride) if with_prev else qidx
        qb = q_r[qidx, :].astype(BF16)
        s = _dot_nt(qb, k_r[kidx, :].astype(BF16))
        s = jnp.where(band_mask if with_prev else causal_mask, s, NEG)
        mx = jnp.max(s, axis=-1, keepdims=True)
        vb = v_r[kidx, :].astype(BF16)
        if first_branch:
            p = jnp.exp(s - mx)
            m_r[qidx, :] = jnp.broadcast_to(mx, (blk, LANES))
            l_r[qidx, :] = jnp.broadcast_to(jnp.sum(p, axis=-1, keepdims=True), (blk, LANES))
            acc_r[qidx, :] = jnp.dot(p.astype(BF16), vb, preferred_element_type=F32)
        else:
            m_old = m_r[qidx, :]
            m_new = jnp.maximum(m_old, mx)
            alpha = jnp.exp(m_old - m_new)
            p = jnp.exp(s - (jnp.concatenate([m_new, m_new], axis=-1) if with_prev else m_new))
            m_r[qidx, :] = m_new
            l_r[qidx, :] = alpha * l_r[qidx, :] + jnp.sum(p, axis=-1, keepdims=True)
            acc_r[qidx, :] = alpha * acc_r[qidx, :] + jnp.dot(p.astype(BF16), vb, preferred_element_type=F32)

    natural = ((qn_s, kn_s, v_ref), (m_s, l_s, acc_s))
    folded = ((qf_s, kf_s, vf_s), (mf_s, lf_s, accf_s))

    for i in range(t // blk):
        update(*natural, i * blk, 1, i > 0, True)

    for src, dst in zip(natural[0] + natural[1], folded[0] + folded[1]):
        for r in range(fold):
            dst[r * tf:(r + 1) * tf, :] = src[pl.ds(r, tf, stride=fold), :]

    for r in range(fold):
        for i in range(tf // blk):
            update(*folded, r * tf + i * blk, 1, i > 0, False)

    for r in range(fold):
        for r2 in range(inner):
            for i in range(tf // inner // blk):
                update(*folded, r * tf + r2 + i * blk * inner, inner, i > 0, False)

    accf_s[...] = accf_s[...] / lf_s[...]
    for r in range(fold):
        acc_s[pl.ds(r, tf, stride=fold), :] = accf_s[r * tf:(r + 1) * tf, :]
    y_ref[...] = acc_s[...].astype(y_ref.dtype)


def _attn_prompt(lidx, z, gq, gk, k_stack, v_stack, *, n_heads):
    b, t, n_in = z.shape
    dd = n_heads * HEAD_DIM
    q0 = (n_in - 3 * dd) // HEAD_DIM
    hd = HEAD_DIM
    window_out = pl.BlockSpec((None, None, t * n_heads, hd), lambda bi, h, l: (l[0], bi, 0, 0),
                              pipeline_mode=pl.Buffered(1))
    grid_spec = pltpu.PrefetchScalarGridSpec(
        num_scalar_prefetch=1,
        grid=(b, n_heads),
        in_specs=[pl.BlockSpec((None, t, hd), lambda bi, h, l: (bi, 0, q0 + h)),
                  pl.BlockSpec((None, t, hd), lambda bi, h, l: (bi, 0, q0 + n_heads + h)),
                  pl.BlockSpec((None, t, hd), lambda bi, h, l: (bi, 0, q0 + 2 * n_heads + h)),
                  pl.BlockSpec((None, 1, hd), lambda bi, h, l: (l[0], 0, 0)),
                  pl.BlockSpec((None, 1, hd), lambda bi, h, l: (l[0], 0, 0)),
                  pl.BlockSpec(memory_space=pl.ANY),
                  pl.BlockSpec(memory_space=pl.ANY)],
        out_specs=[pl.BlockSpec((None, t, hd), lambda bi, h, l: (bi, 0, h)), window_out, window_out],
        scratch_shapes=[pltpu.VMEM((t, hd), F32)] * 11,
    )
    return pl.pallas_call(
        functools.partial(_attn_prompt_kernel, n_heads=n_heads),
        grid_spec=grid_spec,
        out_shape=[jax.ShapeDtypeStruct((b, t, dd), BF16),
                   jax.ShapeDtypeStruct(k_stack.shape, F32),
                   jax.ShapeDtypeStruct(v_stack.shape, F32)],
        input_output_aliases={6: 1, 7: 2},
        compiler_params=_cparams(2),
        name="attn_prompt",
    )(lidx, z, z, z, gq, gk, k_stack, v_stack)


def _attn_sample_kernel(l_ref, qkv_ref, kc_ref, vc_ref, gq_ref, gk_ref, y_ref, ko_ref, *, t_valid, n_heads):
    tq = qkv_ref.shape[0]
    hd = HEAD_DIM
    lc = kc_ref.shape[0] // n_heads

    def multiplicity(dist):
        w = jnp.zeros(dist.shape, F32)
        for window, dil in DILATED_CFG:
            hit = (dist >= 0) & (dist <= window) & ((dist & (dil - 1)) == 0)
            w = w + hit.astype(F32)
        return w

    d1 = lc + lax.broadcasted_iota(jnp.int32, (tq, lc), 0) - lax.broadcasted_iota(jnp.int32, (tq, lc), 1)
    w1 = multiplicity(d1)
    d2 = lax.broadcasted_iota(jnp.int32, (tq, tq), 0) - lax.broadcasted_iota(jnp.int32, (tq, tq), 1)
    w2 = multiplicity(d2) * (lax.broadcasted_iota(jnp.int32, (tq, tq), 1) < t_valid).astype(F32)

    for h in range(n_heads):
        q = qkv_ref[:, h * hd:(h + 1) * hd]
        k = qkv_ref[:, (n_heads + h) * hd:(n_heads + h + 1) * hd]
        v = qkv_ref[:, (2 * n_heads + h) * hd:(2 * n_heads + h + 1) * hd]
        qn = (_head_rmsnorm(q, gq_ref[...]) * (HEAD_DIM ** -0.5)).astype(BF16)
        kn = _head_rmsnorm(k, gk_ref[...])
        ko_ref[:, h * hd:(h + 1) * hd] = kn
        head_rows = pl.ds(h, lc, stride=n_heads)
        s1 = jnp.where(w1 > 0, _dot_nt(qn, kc_ref[head_rows, :].astype(BF16)), NEG)
        s2 = jnp.where(w2 > 0, _dot_nt(qn, kn.astype(BF16)), NEG)
        m = jnp.maximum(jnp.max(s1, axis=-1, keepdims=True), jnp.max(s2, axis=-1, keepdims=True))
        p1 = w1 * jnp.exp(s1 - m)
        p2 = w2 * jnp.exp(s2 - m)
        den = jnp.sum(p1, axis=-1, keepdims=True) + jnp.sum(p2, axis=-1, keepdims=True)
        num = (jnp.dot(p1.astype(BF16), vc_ref[head_rows, :].astype(BF16), preferred_element_type=F32)
               + jnp.dot(p2.astype(BF16), v.astype(BF16), preferred_element_type=F32))
        y_ref[:, h * hd:(h + 1) * hd] = (num / den).astype(y_ref.dtype)


def _attn_sample(lidx, z, cache_k, cache_v, gq, gk, *, n_heads, t_valid):
    b, tq, n_in = z.shape
    depth, _, lc = cache_k.shape[:3]
    hd = HEAD_DIM
    dd = n_heads * hd
    assert (n_in - 3 * dd) % (3 * dd) == 0
    qkv_block = (n_in - 3 * dd) // (3 * dd)
    for window, dil in DILATED_CFG:
        assert dil & (dil - 1) == 0 and window <= lc
    ck = cache_k.reshape(depth, b, lc * n_heads, hd)
    cv = cache_v.reshape(depth, b, lc * n_heads, hd)
    grid_spec = pltpu.PrefetchScalarGridSpec(
        num_scalar_prefetch=1,
        grid=(b,),
        in_specs=[pl.BlockSpec((None, tq, 3 * dd), lambda bi, l: (bi, 0, qkv_block)),
                  pl.BlockSpec((None, None, lc * n_heads, hd), lambda bi, l: (l[0], bi, 0, 0)),
                  pl.BlockSpec((None, None, lc * n_heads, hd), lambda bi, l: (l[0], bi, 0, 0)),
                  pl.BlockSpec((None, 1, hd), lambda bi, l: (l[0], 0, 0)),
                  pl.BlockSpec((None, 1, hd), lambda bi, l: (l[0], 0, 0))],
        out_specs=[pl.BlockSpec((None, tq, dd), lambda bi, l: (bi, 0, 0)),
                   pl.BlockSpec((None, tq, dd), lambda bi, l: (bi, 0, 0))],
    )
    return pl.pallas_call(
        functools.partial(_attn_sample_kernel, t_valid=t_valid, n_heads=n_heads),
        grid_spec=grid_spec,
        out_shape=[jax.ShapeDtypeStruct((b, tq, dd), F32),
                   jax.ShapeDtypeStruct((b, tq, dd), F32)],
        compiler_params=_cparams(1),
        name="attn_sample",
    )(lidx, z, ck, cv, gq, gk)


SAMPLE_ROWS = 8
TM = 1024
TN_UP = 256
TN_WIDE = 512


def kernel(x_prompt, x_sample, state_pool, state_conv_b, state_conv_c, cache_k, cache_v, g_ffn1, w_ffn1_gate, w_ffn1_up, w_ffn1_down, g_mix, w_in, w_pool, pool_scale, w_dw_b, b_dw_b, ln_b_g, ln_b_b, w_dw_c, q_norm_g, k_norm_g, w_out, g_ffn2, w_ffn2_gate, w_ffn2_up, w_ffn2_down):
    bp, tp, d = x_prompt.shape
    bs, ts, _ = x_sample.shape
    depth = g_ffn1.shape[0]
    dg = pool_scale.shape[-1]
    n_heads = cache_k.shape[3]
    n_in = w_in.shape[-1]
    assert ts <= SAMPLE_ROWS and state_pool.shape[2] == POOL_BUF

    row = lambda a: a.reshape(depth, 1, a.shape[-1])
    g1, gm, g2 = row(g_ffn1), row(g_mix), row(g_ffn2)
    taps = lambda a: jnp.broadcast_to(a[:, :, None, :], a.shape[:2] + (SUBLANES, a.shape[-1]))
    mix_w = (w_pool, row(pool_scale), taps(w_dw_b), row(b_dw_b), row(ln_b_g), row(ln_b_b), taps(w_dw_c))
    gq, gk = row(q_norm_g), row(k_norm_g)

    xp, xgp, ssqp = _prep(x_prompt.reshape(bp * tp, d), g1[0], 256)
    xs, xgs, ssqs = _prep(x_sample.reshape(bs * ts, d), g1[0], bs * ts)
    k_stack = lax.empty((depth, bp, tp * n_heads, HEAD_DIM), F32)
    v_stack = lax.empty((depth, bp, tp * n_heads, HEAD_DIM), F32)

    def ffn(lidx, st, wg, wu, wd, g_next, tag):
        xp, xgp, ssqp, xs, xgs, ssqs = st
        ap, a_s = _normed_matmul(lidx, xgp, ssqp, xgs, ssqs, (wg, wu), tm=TM, tn=TN_UP, out_dtype=BF16,
                                 name="ffn_up" + tag)
        return tuple(_residual_matmul(lidx, [ap], [a_s], wd, [0], xp, xs, g_next, scale=0.5, tm=TM, tn=TN_UP,
                                      stacked=False, name="ffn_down" + tag))

    def layer_step(carry, l):
        st, k_stack, v_stack = carry
        lnext = jnp.minimum(l + 1, depth - 1)
        l_same = jnp.stack([l, l]).astype(jnp.int32)
        l_next = jnp.stack([l, lnext]).astype(jnp.int32)

        st = ffn(l_same, st, w_ffn1_gate, w_ffn1_up, w_ffn1_down, gm, "1")
        xp, xgp, ssqp, xs, xgs, ssqs = st
        zp, zs = _normed_matmul(l_same, xgp, ssqp, xgs, ssqs, (w_in,), tm=TM, tn=TN_WIDE, out_dtype=F32,
                                name="proj_in")

        zp = zp.reshape(bp, tp, n_in)
        yabc, pool_p, convb_p, convc_p = _mix_abc(
            l_same, zp, None, mix_w, tt=128, tr=16, t_valid=128, n_prev=0, out_dtype=BF16, name="mix_abc_p")
        yd, k_stack, v_stack = _attn_prompt(l_same, zp, gq, gk, k_stack, v_stack, n_heads=n_heads)

        zs = jnp.pad(zs.reshape(bs, ts, n_in), ((0, 0), (0, SAMPLE_ROWS - ts), (0, 0)))
        yabc_s, pool_s, convb_s, convc_s = _mix_abc(
            l_same, zs, (state_pool, state_conv_b, state_conv_c), mix_w, tt=SAMPLE_ROWS, tr=SAMPLE_ROWS,
            t_valid=ts, n_prev=POOL_BUF, out_dtype=F32, name="mix_abc_s")
        yd_s, k_s = _attn_sample(l_same, zs, cache_k, cache_v, gq, gk, n_heads=n_heads, t_valid=ts)
        v_s = zs[:, :ts, n_in - dg:]

        a_p = [yabc.reshape(bp * tp, 3 * dg), yd.reshape(bp * tp, dg)]
        a_s = [yabc_s[:, :ts].reshape(bs * ts, 3 * dg).astype(BF16), yd_s[:, :ts].reshape(bs * ts, dg).astype(BF16)]
        st = tuple(_residual_matmul(l_same, a_p, a_s, w_out, [0, 3], xp, xs, g2, scale=1.0, tm=TM, tn=TN_WIDE,
                                    stacked=True, name="proj_out"))
        st = ffn(l_next, st, w_ffn2_gate, w_ffn2_up, w_ffn2_down, g1, "2")

        outs = (pool_p, pool_s, convb_p, convb_s, convc_p, convc_s,
                k_s[:, :ts].reshape(bs, ts, n_heads, HEAD_DIM), v_s.reshape(bs, ts, n_heads, HEAD_DIM))
        return (st, k_stack, v_stack), outs

    init = ((xp, xgp, ssqp, xs, xgs, ssqs), k_stack, v_stack)
    (st, k_stack, v_stack), outs = lax.scan(layer_step, init, jnp.arange(depth, dtype=jnp.int32))
    window = lambda a: a.reshape(depth, bp, tp, n_heads, HEAD_DIM)
    return ((st[0].reshape(bp, tp, d), st[3].reshape(bs, ts, d)) + tuple(outs[:6])
            + (window(k_stack), window(v_stack)) + tuple(outs[6:]))
```

```python
import functools

import jax
import jax.numpy as jnp
from jax import lax
from jax.experimental import pallas as pl
from jax.experimental.pallas import tpu as pltpu

EPS = 1e-6
NEG = -1e30
LANES = 128
SUBLANES = 8
HEAD_DIM = 128
POOL_WINDOWS = (2, 4, 8, 16)
POOL_BUF = max(POOL_WINDOWS) - 1
CONV_B_WIDTH = 31
CONV_C_WIDTH = 3
DILATED_CFG = ((128, 1), (512, 4), (2048, 16))
ATT_BLK = 128
VMEM_LIMIT = 60 * 1024 * 1024

F32 = jnp.float32
BF16 = jnp.bfloat16


def _cparams(n_axes):
    return pltpu.CompilerParams(dimension_semantics=("arbitrary",) * n_axes,
                                vmem_limit_bytes=VMEM_LIMIT)


def _lane_fold(v):
    n = v.shape[-1] // LANES
    out = v[:, 0:LANES]
    for k in range(1, n):
        out = out + v[:, k * LANES:(k + 1) * LANES]
    return out


def _row_rms_scale(ssq, d_model):
    return lax.rsqrt(jnp.sum(ssq, axis=-1, keepdims=True) * (1.0 / d_model) + EPS)


def _prep_kernel(x_ref, g_ref, x_out, xg_ref, ssq_ref):
    x = x_ref[...]
    x_out[...] = x
    xg_ref[...] = (x * g_ref[...]).astype(BF16)
    ssq_ref[...] = _lane_fold(x * x)


def _prep(x, g_row, tm):
    m, d = x.shape
    return pl.pallas_call(
        _prep_kernel,
        grid=(m // tm,),
        in_specs=[pl.BlockSpec((tm, d), lambda i: (i, 0)),
                  pl.BlockSpec((1, d), lambda i: (0, 0))],
        out_specs=[pl.BlockSpec((tm, d), lambda i: (i, 0)),
                   pl.BlockSpec((tm, d), lambda i: (i, 0)),
                   pl.BlockSpec((tm, LANES), lambda i: (i, 0))],
        out_shape=[jax.ShapeDtypeStruct((m, d), F32),
                   jax.ShapeDtypeStruct((m, d), BF16),
                   jax.ShapeDtypeStruct((m, LANES), F32)],
        compiler_params=_cparams(1),
        name="norm_prep",
    )(x, g_row)


def _side_col(nj):
    return lambda i, j: jnp.where(i == 0, j, nj - 1)


def _up_kernel(l_ref, xg_ref, ssq_ref, xgs_ref, ssqs_ref, *rest, n_w, d_model):
    w_refs, (o_ref, os_ref, lhs_s) = rest[:n_w], rest[n_w:]
    tm = xg_ref.shape[0]
    i, j = pl.program_id(0), pl.program_id(1)

    def weights():
        return [w[...].astype(BF16) for w in w_refs]

    def act(outs):
        return outs[0] if n_w == 1 else jax.nn.silu(outs[0]) * outs[1]

    @pl.when((i == 0) & (j == 0))
    def _():
        lhs_s[0:tm, :] = xg_ref[...]
        lhs_s[tm:, :] = xgs_ref[...]

    @pl.when(i == 0)
    def _():
        r = _row_rms_scale(ssq_ref[...], d_model)
        rs = _row_rms_scale(ssqs_ref[...], d_model)
        lhs = lhs_s[...]
        outs = [jnp.dot(lhs, w, preferred_element_type=F32) for w in weights()]
        o_ref[...] = act([o[0:tm] * r for o in outs]).astype(o_ref.dtype)
        os_ref[...] = act([o[tm:] * rs for o in outs]).astype(os_ref.dtype)

    @pl.when(i > 0)
    def _():
        r = _row_rms_scale(ssq_ref[...], d_model)
        xg = xg_ref[...]
        o_ref[...] = act([jnp.dot(xg, w, preferred_element_type=F32) * r for w in weights()]).astype(o_ref.dtype)


def _normed_matmul(lidx, xg, ssq, xgs, ssqs, ws, *, tm, tn, out_dtype, name):
    m, k = xg.shape
    ms = xgs.shape[0]
    n = ws[0].shape[-1]
    sj = _side_col(n // tn)
    grid_spec = pltpu.PrefetchScalarGridSpec(
        num_scalar_prefetch=1,
        grid=(m // tm, n // tn),
        in_specs=[pl.BlockSpec((tm, k), lambda i, j, l: (i, 0)),
                  pl.BlockSpec((tm, LANES), lambda i, j, l: (i, 0)),
                  pl.BlockSpec((ms, k), lambda i, j, l: (0, 0)),
                  pl.BlockSpec((ms, LANES), lambda i, j, l: (0, 0))]
                 + [pl.BlockSpec((None, k, tn), lambda i, j, l: (l[0], 0, j)) for _ in ws],
        out_specs=[pl.BlockSpec((tm, tn), lambda i, j, l: (i, j)),
                   pl.BlockSpec((ms, tn), lambda i, j, l: (0, sj(i, j)))],
        scratch_shapes=[pltpu.VMEM((tm + ms, k), BF16)],
    )
    return pl.pallas_call(
        functools.partial(_up_kernel, n_w=len(ws), d_model=k),
        grid_spec=grid_spec,
        out_shape=[jax.ShapeDtypeStruct((m, n), out_dtype),
                   jax.ShapeDtypeStruct((ms, n), out_dtype)],
        compiler_params=_cparams(2),
        name=name,
    )(lidx, xg, ssq, xgs, ssqs, *ws)


def _down_kernel(l_ref, *refs, n_a, scale, stacked):
    a_refs, as_refs, w_refs = refs[:n_a], refs[n_a:2 * n_a], refs[2 * n_a:3 * n_a]
    res_ref, ress_ref, g_ref, x_out, xg_out, ssq_out, xs_out, xgs_out, ssqs_out = refs[3 * n_a:3 * n_a + 9]
    lhs_refs = refs[3 * n_a + 9:]
    tm = res_ref.shape[0]
    i, j = pl.program_id(0), pl.program_id(1)
    first_col = j == 0

    def weights():
        return [w[...].astype(BF16) for w in w_refs]

    def matmul(lhs_group, wb):
        acc = None
        for a, w in zip(lhs_group, wb):
            part = jnp.dot(a[...], w, preferred_element_type=F32)
            acc = part if acc is None else acc + part
        return acc

    def finish(acc, res, x_o, xg_o, ssq_o):
        xn = res[...] + scale * acc
        x_o[...] = xn
        xg_o[...] = (xn * g_ref[...]).astype(BF16)
        part_ssq = _lane_fold(xn * xn)

        @pl.when(first_col)
        def _():
            ssq_o[...] = part_ssq

        @pl.when(jnp.logical_not(first_col))
        def _():
            ssq_o[...] += part_ssq

    if not stacked:
        wb = weights()
        finish(matmul(a_refs, wb), res_ref, x_out, xg_out, ssq_out)

        @pl.when(i == 0)
        def _():
            finish(matmul(as_refs, wb), ress_ref, xs_out, xgs_out, ssqs_out)
        return

    @pl.when((i == 0) & first_col)
    def _():
        for lhs, a, a_s in zip(lhs_refs, a_refs, as_refs):
            lhs[0:tm, :] = a[...]
            lhs[tm:, :] = a_s[...]

    @pl.when(i == 0)
    def _():
        acc = matmul(lhs_refs, weights())
        finish(acc[0:tm], res_ref, x_out, xg_out, ssq_out)
        finish(acc[tm:], ress_ref, xs_out, xgs_out, ssqs_out)

    @pl.when(i > 0)
    def _():
        finish(matmul(a_refs, weights()), res_ref, x_out, xg_out, ssq_out)


def _residual_matmul(lidx, a_list, as_list, w, row_blocks, res, ress, g_next, *, scale, tm, tn, stacked, name):
    m, ms = res.shape[0], ress.shape[0]
    n = w.shape[-1]
    sj = _side_col(n // tn)
    in_specs = [pl.BlockSpec((tm, a.shape[1]), lambda i, j, l: (i, 0), pipeline_mode=pl.Buffered(1))
                for a in a_list]
    in_specs += [pl.BlockSpec((ms, a.shape[1]), lambda i, j, l: (0, 0)) for a in as_list]
    in_specs += [pl.BlockSpec((None, a.shape[1], tn), functools.partial(lambda i, j, l, rb: (l[0], rb, j), rb=rb))
                 for a, rb in zip(a_list, row_blocks)]
    in_specs += [pl.BlockSpec((tm, tn), lambda i, j, l: (i, j)),
                 pl.BlockSpec((ms, tn), lambda i, j, l: (0, sj(i, j))),
                 pl.BlockSpec((None, 1, tn), lambda i, j, l: (l[1], 0, j))]
    grid_spec = pltpu.PrefetchScalarGridSpec(
        num_scalar_prefetch=1,
        grid=(m // tm, n // tn),
        in_specs=in_specs,
        out_specs=[pl.BlockSpec((tm, tn), lambda i, j, l: (i, j)),
                   pl.BlockSpec((tm, tn), lambda i, j, l: (i, j)),
                   pl.BlockSpec((tm, LANES), lambda i, j, l: (i, 0)),
                   pl.BlockSpec((ms, tn), lambda i, j, l: (0, sj(i, j))),
                   pl.BlockSpec((ms, tn), lambda i, j, l: (0, sj(i, j))),
                   pl.BlockSpec((ms, LANES), lambda i, j, l: (0, 0))],
        scratch_shapes=[pltpu.VMEM((tm + ms, a.shape[1]), BF16) for a in a_list] if stacked else [],
    )
    return pl.pallas_call(
        functools.partial(_down_kernel, n_a=len(a_list), scale=scale, stacked=stacked),
        grid_spec=grid_spec,
        out_shape=[jax.ShapeDtypeStruct((m, n), F32),
                   jax.ShapeDtypeStruct((m, n), BF16),
                   jax.ShapeDtypeStruct((m, LANES), F32),
                   jax.ShapeDtypeStruct((ms, n), F32),
                   jax.ShapeDtypeStruct((ms, n), BF16),
                   jax.ShapeDtypeStruct((ms, LANES), F32)],
        compiler_params=_cparams(2),
        name=name,
    )(lidx, *a_list, *as_list, *([w] * len(a_list)), res, ress, g_next)


HIST_A = 16
HIST_B = 32
HIST_C = 8


def _mix_abc_kernel(l_ref, *refs, tt, tr, t_valid, n_prev, n_chunks, has_state, dg):
    if has_state:
        (z_ref, sp_ref, sb_ref, sc_ref, wpool_ref, pscale_ref, wdwb_ref, bdwb_ref, lng_ref, lnb_ref, wdwc_ref,
         y_ref, np_ref, nb_ref, nc_ref, full_a, full_b, full_c, shift_b, pooled_s) = refs
    else:
        (z_ref, wpool_ref, pscale_ref, wdwb_ref, bdwb_ref, lng_ref, lnb_ref, wdwc_ref,
         y_ref, np_ref, nb_ref, nc_ref, full_a, full_b, full_c, shift_b, pooled_s) = refs
    c = pl.program_id(1)

    @pl.when(c == 0)
    def _():
        full_a[0:HIST_A, :] = jnp.zeros((HIST_A, dg), F32)
        full_b[0:HIST_B, :] = jnp.zeros((HIST_B, dg), F32)
        full_c[0:HIST_C, :] = jnp.zeros((HIST_C, dg), F32)
        if has_state:
            full_a[HIST_A - POOL_BUF:HIST_A, :] = sp_ref[...]
            full_b[HIST_B - (CONV_B_WIDTH - 1):HIST_B, :] = sb_ref[...]
            full_c[HIST_C - (CONV_C_WIDTH - 1):HIST_C, :] = sc_ref[...]

    full_a[HIST_A:HIST_A + tt, :] = z_ref[:, 0:dg]
    full_b[HIST_B:HIST_B + tt, :] = z_ref[:, dg:2 * dg] * jax.nn.sigmoid(z_ref[:, 2 * dg:3 * dg])
    full_c[HIST_C:HIST_C + tt, :] = z_ref[:, 4 * dg:5 * dg] * z_ref[:, 5 * dg:6 * dg]

    n_shift = HIST_B + tt - SUBLANES
    for s in range(1, SUBLANES):
        shift_b[s - 1, :, :] = full_b[s:s + n_shift, :]

    def tap(w_ref, j):
        w = w_ref[j]
        return w if tr == SUBLANES else jnp.concatenate([w] * (tr // SUBLANES), axis=0)

    pch = dg // len(POOL_WINDOWS)
    for r0 in range(0, tt, tr):
        t_glob = c * tt + r0 + lax.broadcasted_iota(jnp.int32, (tr, 1), 0)
        for g, w in enumerate(POOL_WINDOWS):
            cols = slice(g * pch, (g + 1) * pch)
            cur = full_a[HIST_A + r0:HIST_A + r0 + tr, cols]
            acc = cur
            for s in range(1, w):
                acc = acc + full_a[HIST_A + r0 - s:HIST_A + r0 - s + tr, cols]
            cnt = jnp.minimum(w, n_prev + t_glob + 1).astype(F32)
            pooled_s[r0:r0 + tr, cols] = acc / cnt - cur

        acc = jnp.zeros((tr, dg), F32) + bdwb_ref[...]
        for j in range(CONV_B_WIDTH):
            off = HIST_B - (CONV_B_WIDTH - 1) + j
            base, s = r0 + off - off % SUBLANES, off % SUBLANES
            src = full_b[base:base + tr, :] if s == 0 else shift_b[s - 1, base:base + tr, :]
            acc = acc + tap(wdwb_ref, j) * src
        mu = jnp.mean(acc, axis=-1, keepdims=True)
        xc = acc - mu
        var = jnp.mean(xc * xc, axis=-1, keepdims=True)
        yn = xc * lax.rsqrt(var + EPS) * lng_ref[...] + lnb_ref[...]
        y_ref[r0:r0 + tr, dg:2 * dg] = jax.nn.silu(yn).astype(y_ref.dtype)

        acc = jnp.zeros((tr, dg), F32)
        off = HIST_C - (CONV_C_WIDTH - 1) + r0
        for j in range(CONV_C_WIDTH):
            acc = acc + tap(wdwc_ref, j) * full_c[off + j:off + j + tr, :]
        y_ref[r0:r0 + tr, 2 * dg:3 * dg] = (z_ref[r0:r0 + tr, 3 * dg:4 * dg] * acc).astype(y_ref.dtype)

    for g in range(len(POOL_WINDOWS)):
        cols = slice(g * pch, (g + 1) * pch)
        ya = jnp.dot(pooled_s[:, cols].astype(BF16), wpool_ref[g].astype(BF16), preferred_element_type=F32)
        y_ref[:, cols] = (ya * pscale_ref[:, cols]).astype(y_ref.dtype)

    @pl.when(c == n_chunks - 1)
    def _():
        np_ref[...] = full_a[HIST_A + t_valid - POOL_BUF:HIST_A + t_valid, :]
        nb_ref[...] = full_b[HIST_B + t_valid - (CONV_B_WIDTH - 1):HIST_B + t_valid, :]
        nc_ref[...] = full_c[HIST_C + t_valid - (CONV_C_WIDTH - 1):HIST_C + t_valid, :]

    if n_chunks > 1:
        @pl.when(c < n_chunks - 1)
        def _():
            full_a[0:HIST_A, :] = full_a[tt:tt + HIST_A, :]
            full_b[0:HIST_B, :] = full_b[tt:tt + HIST_B, :]
            full_c[0:HIST_C, :] = full_c[tt:tt + HIST_C, :]


def _mix_abc(lidx, z, states, wts, *, tt, tr, t_valid, n_prev, out_dtype, name):
    b, t, _ = z.shape
    w_pool, pool_scale, w_dw_b, b_dw_b, ln_g, ln_b, w_dw_c = wts
    dg = pool_scale.shape[-1]
    n_chunks = t // tt
    has_state = states is not None

    def lsel(*tail):
        return lambda bi, ci, l: (l[0],) + tail

    in_specs = [pl.BlockSpec((None, tt, 6 * dg), lambda bi, ci, l: (bi, ci, 0))]
    args = [z]
    if has_state:
        for s in states:
            in_specs.append(pl.BlockSpec((None, None) + s.shape[2:], lambda bi, ci, l: (l[0], bi, 0, 0)))
            args.append(s)
    in_specs += [
        pl.BlockSpec((None,) + w_pool.shape[1:], lsel(0, 0, 0)),
        pl.BlockSpec((None, 1, dg), lsel(0, 0)),
        pl.BlockSpec((None, CONV_B_WIDTH, SUBLANES, dg), lsel(0, 0, 0)),
        pl.BlockSpec((None, 1, dg), lsel(0, 0)),
        pl.BlockSpec((None, 1, dg), lsel(0, 0)),
        pl.BlockSpec((None, 1, dg), lsel(0, 0)),
        pl.BlockSpec((None, CONV_C_WIDTH, SUBLANES, dg), lsel(0, 0, 0)),
    ]
    args += [w_pool, pool_scale, w_dw_b, b_dw_b, ln_g, ln_b, w_dw_c]
    grid_spec = pltpu.PrefetchScalarGridSpec(
        num_scalar_prefetch=1,
        grid=(b, n_chunks),
        in_specs=in_specs,
        out_specs=[pl.BlockSpec((None, tt, 3 * dg), lambda bi, ci, l: (bi, ci, 0)),
                   pl.BlockSpec((None, POOL_BUF, dg), lambda bi, ci, l: (bi, 0, 0)),
                   pl.BlockSpec((None, CONV_B_WIDTH - 1, dg), lambda bi, ci, l: (bi, 0, 0)),
                   pl.BlockSpec((None, CONV_C_WIDTH - 1, dg), lambda bi, ci, l: (bi, 0, 0))],
        scratch_shapes=[pltpu.VMEM((HIST_A + tt, dg), F32),
                        pltpu.VMEM((HIST_B + tt, dg), F32),
                        pltpu.VMEM((HIST_C + tt, dg), F32),
                        pltpu.VMEM((SUBLANES - 1, HIST_B + tt - SUBLANES, dg), F32),
                        pltpu.VMEM((tt, dg), F32)],
    )
    return pl.pallas_call(
        functools.partial(_mix_abc_kernel, tt=tt, tr=tr, t_valid=t_valid, n_prev=n_prev,
                          n_chunks=n_chunks, has_state=has_state, dg=dg),
        grid_spec=grid_spec,
        out_shape=[jax.ShapeDtypeStruct((b, t, 3 * dg), out_dtype),
                   jax.ShapeDtypeStruct((b, POOL_BUF, dg), F32),
                   jax.ShapeDtypeStruct((b, CONV_B_WIDTH - 1, dg), F32),
                   jax.ShapeDtypeStruct((b, CONV_C_WIDTH - 1, dg), F32)],
        compiler_params=_cparams(2),
        name=name,
    )(lidx, *args)


def _head_rmsnorm(x, g):
    return x * lax.rsqrt(jnp.mean(x * x, axis=-1, keepdims=True) + EPS) * g


def _dot_nt(a, b):
    return lax.dot_general(a, b, (((1,), (1,)), ((), ())), preferred_element_type=F32)


def _attn_prompt_kernel(l_ref, q_ref, k_ref, v_ref, gq_ref, gk_ref, kstack_ref, vstack_ref, y_ref, ko_ref, vo_ref,
                        qn_s, kn_s, m_s, l_s, acc_s, qf_s, kf_s, vf_s, mf_s, lf_s, accf_s, *, n_heads):
    t = q_ref.shape[0]
    blk = ATT_BLK
    (w0, d0), (w1, d1), (w2, d2) = DILATED_CFG
    assert d0 == 1 and d2 % d1 == 0 and w0 // d0 == blk and w1 // d1 == blk and w2 // d2 == blk
    assert (t // d2) % blk == 0
    fold, inner, tf = d1, d2 // d1, t // d1

    qn_s[...] = _head_rmsnorm(q_ref[...], gq_ref[...]) * (HEAD_DIM ** -0.5)
    kn_s[...] = _head_rmsnorm(k_ref[...], gk_ref[...])
    head_rows = pl.ds(pl.program_id(1), t, stride=n_heads)
    ko_ref[head_rows, :] = kn_s[...]
    vo_ref[head_rows, :] = v_ref[...]

    row1 = lax.broadcasted_iota(jnp.int32, (blk, blk), 0)
    col1 = lax.broadcasted_iota(jnp.int32, (blk, blk), 1)
    causal_mask = col1 <= row1
    row2 = lax.broadcasted_iota(jnp.int32, (blk, 2 * blk), 0)
    col2 = lax.broadcasted_iota(jnp.int32, (blk, 2 * blk), 1)
    band_mask = (col2 >= row2) & (col2 <= row2 + blk)

    def rows(start, n, stride):
        return pl.ds(start, n, stride=stride) if stride > 1 else pl.ds(start, n)

    def update(qkv, state, start, stride, with_prev, first_branch, old=None):
        q_r, k_r, v_r = qkv
        m_r, l_r, acc_r = state
        qidx = rows(start, blk, stride)
        kidx = rows(start - blk * stride, 2 * blk, stride) if with_prev else qidx
        qb = q_r[qidx, :].astype(BF16)
        s = _dot_nt(qb, k_r[kidx, :].astype(BF16))
        s = jnp.where(band_mask if with_prev else causal_mask, s, NEG)
        mx = jnp.max(s, axis=-1, keepdims=True)
        vb = v_r[kidx, :].astype(BF16)
        if first_branch:
            p = jnp.exp(s - mx)
            m_r[qidx, :] = jnp.broadcast_to(mx, (blk, LANES))
            l_r[qidx, :] = jnp.broadcast_to(jnp.sum(p, axis=-1, keepdims=True), (blk, LANES))
            acc_r[qidx, :] = jnp.dot(p.astype(BF16), vb, preferred_element_type=F32)
        else:
            (mo_r, lo_r, acco_r), oidx = (state, qidx) if old is None else old
            m_old = mo_r[oidx, :]
            m_new = jnp.maximum(m_old, mx)
            alpha = jnp.exp(m_old - m_new)
            p = jnp.exp(s - (jnp.concatenate([m_new, m_new], axis=-1) if with_prev else m_new))
            m_r[qidx, :] = m_new
            l_r[qidx, :] = alpha * lo_r[oidx, :] + jnp.sum(p, axis=-1, keepdims=True)
            acc_r[qidx, :] = alpha * acco_r[oidx, :] + jnp.dot(p.astype(BF16), vb, preferred_element_type=F32)

    natural = ((qn_s, kn_s, v_ref), (m_s, l_s, acc_s))
    folded = ((qf_s, kf_s, vf_s), (mf_s, lf_s, accf_s))

    for i in range(t // blk):
        update(*natural, i * blk, 1, i > 0, True)

    for src, dst in zip(natural[0], folded[0]):
        for r in range(fold):
            dst[r * tf:(r + 1) * tf, :] = src[pl.ds(r, tf, stride=fold), :]

    for r in range(fold):
        for i in range(tf // blk):
            update(*folded, r * tf + i * blk, 1, i > 0, False,
                   old=(natural[1], pl.ds(r + i * blk * fold, blk, stride=fold)))

    for r in range(fold):
        for r2 in range(inner):
            for i in range(tf // inner // blk):
                update(*folded, r * tf + r2 + i * blk * inner, inner, i > 0, False)

    accf_s[...] = accf_s[...] / lf_s[...]
    for r in range(fold):
        acc_s[pl.ds(r, tf, stride=fold), :] = accf_s[r * tf:(r + 1) * tf, :]
    y_ref[...] = acc_s[...].astype(y_ref.dtype)


def _attn_prompt(lidx, z, gq, gk, k_stack, v_stack, *, n_heads):
    b, t, n_in = z.shape
    dd = n_heads * HEAD_DIM
    q0 = (n_in - 3 * dd) // HEAD_DIM
    hd = HEAD_DIM
    window_out = pl.BlockSpec((None, None, t * n_heads, hd), lambda bi, h, l: (l[0], bi, 0, 0))
    grid_spec = pltpu.PrefetchScalarGridSpec(
        num_scalar_prefetch=1,
        grid=(b, n_heads),
        in_specs=[pl.BlockSpec((None, t, hd), lambda bi, h, l: (bi, 0, q0 + h)),
                  pl.BlockSpec((None, t, hd), lambda bi, h, l: (bi, 0, q0 + n_heads + h)),
                  pl.BlockSpec((None, t, hd), lambda bi, h, l: (bi, 0, q0 + 2 * n_heads + h)),
                  pl.BlockSpec((None, 1, hd), lambda bi, h, l: (l[0], 0, 0)),
                  pl.BlockSpec((None, 1, hd), lambda bi, h, l: (l[0], 0, 0)),
                  pl.BlockSpec(memory_space=pl.ANY),
                  pl.BlockSpec(memory_space=pl.ANY)],
        out_specs=[pl.BlockSpec((None, t, hd), lambda bi, h, l: (bi, 0, h)), window_out, window_out],
        scratch_shapes=[pltpu.VMEM((t, hd), F32)] * 11,
    )
    return pl.pallas_call(
        functools.partial(_attn_prompt_kernel, n_heads=n_heads),
        grid_spec=grid_spec,
        out_shape=[jax.ShapeDtypeStruct((b, t, dd), BF16),
                   jax.ShapeDtypeStruct(k_stack.shape, F32),
                   jax.ShapeDtypeStruct(v_stack.shape, F32)],
        input_output_aliases={6: 1, 7: 2},
        compiler_params=_cparams(2),
        name="attn_prompt",
    )(lidx, z, z, z, gq, gk, k_stack, v_stack)


def _attn_sample_kernel(l_ref, qkv_ref, kc_ref, vc_ref, gq_ref, gk_ref, y_ref, ko_ref, *, t_valid, n_heads):
    tq = qkv_ref.shape[0]
    hd = HEAD_DIM
    lc = kc_ref.shape[0] // n_heads

    def multiplicity(dist):
        w = jnp.zeros(dist.shape, F32)
        for window, dil in DILATED_CFG:
            hit = (dist >= 0) & (dist <= window) & ((dist & (dil - 1)) == 0)
            w = w + hit.astype(F32)
        return w

    d1 = lc + lax.broadcasted_iota(jnp.int32, (tq, lc), 0) - lax.broadcasted_iota(jnp.int32, (tq, lc), 1)
    w1 = multiplicity(d1)
    d2 = lax.broadcasted_iota(jnp.int32, (tq, tq), 0) - lax.broadcasted_iota(jnp.int32, (tq, tq), 1)
    w2 = multiplicity(d2) * (lax.broadcasted_iota(jnp.int32, (tq, tq), 1) < t_valid).astype(F32)

    for h in range(n_heads):
        q = qkv_ref[:, h * hd:(h + 1) * hd]
        k = qkv_ref[:, (n_heads + h) * hd:(n_heads + h + 1) * hd]
        v = qkv_ref[:, (2 * n_heads + h) * hd:(2 * n_heads + h + 1) * hd]
        qn = (_head_rmsnorm(q, gq_ref[...]) * (HEAD_DIM ** -0.5)).astype(BF16)
        kn = _head_rmsnorm(k, gk_ref[...])
        ko_ref[:, h * hd:(h + 1) * hd] = kn
        head_rows = pl.ds(h, lc, stride=n_heads)
        s1 = jnp.where(w1 > 0, _dot_nt(qn, kc_ref[head_rows, :].astype(BF16)), NEG)
        s2 = jnp.where(w2 > 0, _dot_nt(qn, kn.astype(BF16)), NEG)
        m = jnp.maximum(jnp.max(s1, axis=-1, keepdims=True), jnp.max(s2, axis=-1, keepdims=True))
        p1 = w1 * jnp.exp(s1 - m)
        p2 = w2 * jnp.exp(s2 - m)
        den = jnp.sum(p1, axis=-1, keepdims=True) + jnp.sum(p2, axis=-1, keepdims=True)
        num = (jnp.dot(p1.astype(BF16), vc_ref[head_rows, :].astype(BF16), preferred_element_type=F32)
               + jnp.dot(p2.astype(BF16), v.astype(BF16), preferred_element_type=F32))
        y_ref[:, h * hd:(h + 1) * hd] = (num / den).astype(y_ref.dtype)


def _attn_sample(lidx, z, cache_k, cache_v, gq, gk, *, n_heads, t_valid):
    b, tq, n_in = z.shape
    depth, _, lc = cache_k.shape[:3]
    hd = HEAD_DIM
    dd = n_heads * hd
    assert (n_in - 3 * dd) % (3 * dd) == 0
    qkv_block = (n_in - 3 * dd) // (3 * dd)
    for window, dil in DILATED_CFG:
        assert dil & (dil - 1) == 0 and window <= lc
    ck = cache_k.reshape(depth, b, lc * n_heads, hd)
    cv = cache_v.reshape(depth, b, lc * n_heads, hd)
    grid_spec = pltpu.PrefetchScalarGridSpec(
        num_scalar_prefetch=1,
        grid=(b,),
        in_specs=[pl.BlockSpec((None, tq, 3 * dd), lambda bi, l: (bi, 0, qkv_block)),
                  pl.BlockSpec((None, None, lc * n_heads, hd), lambda bi, l: (l[0], bi, 0, 0)),
                  pl.BlockSpec((None, None, lc * n_heads, hd), lambda bi, l: (l[0], bi, 0, 0)),
                  pl.BlockSpec((None, 1, hd), lambda bi, l: (l[0], 0, 0)),
                  pl.BlockSpec((None, 1, hd), lambda bi, l: (l[0], 0, 0))],
        out_specs=[pl.BlockSpec((None, tq, dd), lambda bi, l: (bi, 0, 0)),
                   pl.BlockSpec((None, tq, dd), lambda bi, l: (bi, 0, 0))],
    )
    return pl.pallas_call(
        functools.partial(_attn_sample_kernel, t_valid=t_valid, n_heads=n_heads),
        grid_spec=grid_spec,
        out_shape=[jax.ShapeDtypeStruct((b, tq, dd), F32),
                   jax.ShapeDtypeStruct((b, tq, dd), F32)],
        compiler_params=_cparams(1),
        name="attn_sample",
    )(lidx, z, ck, cv, gq, gk)


SAMPLE_ROWS = 8
TM = 1024
TN_UP = 256
TN_WIDE = 512


def kernel(x_prompt, x_sample, state_pool, state_conv_b, state_conv_c, cache_k, cache_v, g_ffn1, w_ffn1_gate, w_ffn1_up, w_ffn1_down, g_mix, w_in, w_pool, pool_scale, w_dw_b, b_dw_b, ln_b_g, ln_b_b, w_dw_c, q_norm_g, k_norm_g, w_out, g_ffn2, w_ffn2_gate, w_ffn2_up, w_ffn2_down):
    bp, tp, d = x_prompt.shape
    bs, ts, _ = x_sample.shape
    depth = g_ffn1.shape[0]
    dg = pool_scale.shape[-1]
    n_heads = cache_k.shape[3]
    n_in = w_in.shape[-1]
    assert ts <= SAMPLE_ROWS and state_pool.shape[2] == POOL_BUF

    row = lambda a: a.reshape(depth, 1, a.shape[-1])
    g1, gm, g2 = row(g_ffn1), row(g_mix), row(g_ffn2)
    taps = lambda a: jnp.broadcast_to(a[:, :, None, :], a.shape[:2] + (SUBLANES, a.shape[-1]))
    mix_w = (w_pool, row(pool_scale), taps(w_dw_b), row(b_dw_b), row(ln_b_g), row(ln_b_b), taps(w_dw_c))
    gq, gk = row(q_norm_g), row(k_norm_g)

    xp, xgp, ssqp = _prep(x_prompt.reshape(bp * tp, d), g1[0], 256)
    xs, xgs, ssqs = _prep(x_sample.reshape(bs * ts, d), g1[0], bs * ts)
    k_stack = lax.empty((depth, bp, tp * n_heads, HEAD_DIM), F32)
    v_stack = lax.empty((depth, bp, tp * n_heads, HEAD_DIM), F32)

    def ffn(lidx, st, wg, wu, wd, g_next, tag):
        xp, xgp, ssqp, xs, xgs, ssqs = st
        ap, a_s = _normed_matmul(lidx, xgp, ssqp, xgs, ssqs, (wg, wu), tm=TM, tn=TN_UP, out_dtype=BF16,
                                 name="ffn_up" + tag)
        return tuple(_residual_matmul(lidx, [ap], [a_s], wd, [0], xp, xs, g_next, scale=0.5, tm=TM, tn=TN_UP,
                                      stacked=False, name="ffn_down" + tag))

    def layer_step(carry, l):
        st, k_stack, v_stack = carry
        lnext = jnp.minimum(l + 1, depth - 1)
        l_same = jnp.stack([l, l]).astype(jnp.int32)
        l_next = jnp.stack([l, lnext]).astype(jnp.int32)

        st = ffn(l_same, st, w_ffn1_gate, w_ffn1_up, w_ffn1_down, gm, "1")
        xp, xgp, ssqp, xs, xgs, ssqs = st
        zp, zs = _normed_matmul(l_same, xgp, ssqp, xgs, ssqs, (w_in,), tm=TM, tn=TN_WIDE, out_dtype=F32,
                                name="proj_in")

        zp = zp.reshape(bp, tp, n_in)
        yabc, pool_p, convb_p, convc_p = _mix_abc(
            l_same, zp, None, mix_w, tt=128, tr=16, t_valid=128, n_prev=0, out_dtype=BF16, name="mix_abc_p")
        yd, k_stack, v_stack = _attn_prompt(l_same, zp, gq, gk, k_stack, v_stack, n_heads=n_heads)

        zs = jnp.pad(zs.reshape(bs, ts, n_in), ((0, 0), (0, SAMPLE_ROWS - ts), (0, 0)))
        yabc_s, pool_s, convb_s, convc_s = _mix_abc(
            l_same, zs, (state_pool, state_conv_b, state_conv_c), mix_w, tt=SAMPLE_ROWS, tr=SAMPLE_ROWS,
            t_valid=ts, n_prev=POOL_BUF, out_dtype=F32, name="mix_abc_s")
        yd_s, k_s = _attn_sample(l_same, zs, cache_k, cache_v, gq, gk, n_heads=n_heads, t_valid=ts)
        v_s = zs[:, :ts, n_in - dg:]

        a_p = [yabc.reshape(bp * tp, 3 * dg), yd.reshape(bp * tp, dg)]
        a_s = [yabc_s[:, :ts].reshape(bs * ts, 3 * dg).astype(BF16), yd_s[:, :ts].reshape(bs * ts, dg).astype(BF16)]
        st = tuple(_residual_matmul(l_same, a_p, a_s, w_out, [0, 3], xp, xs, g2, scale=1.0, tm=TM, tn=TN_WIDE,
                                    stacked=True, name="proj_out"))
        st = ffn(l_next, st, w_ffn2_gate, w_ffn2_up, w_ffn2_down, g1, "2")

        outs = (pool_p, pool_s, convb_p, convb_s, convc_p, convc_s,
                k_s[:, :ts].reshape(bs, ts, n_heads, HEAD_DIM), v_s.reshape(bs, ts, n_heads, HEAD_DIM))
        return (st, k_stack, v_stack), outs

    init = ((xp, xgp, ssqp, xs, xgs, ssqs), k_stack, v_stack)
    (st, k_stack, v_stack), outs = lax.scan(layer_step, init, jnp.arange(depth, dtype=jnp.int32))
    window = lambda a: a.reshape(depth, bp, tp, n_heads, HEAD_DIM)
    return ((st[0].reshape(bp, tp, d), st[3].reshape(bs, ts, d)) + tuple(outs[:6])
            + (window(k_stack), window(v_stack)) + tuple(outs[6:]))
```

```python
import functools

import jax
import jax.numpy as jnp
from jax import lax
from jax.experimental import pallas as pl
from jax.experimental.pallas import tpu as pltpu

EPS = 1e-6
NEG = -1e30
LANES = 128
SUBLANES = 8
HEAD_DIM = 128
POOL_WINDOWS = (2, 4, 8, 16)
POOL_BUF = max(POOL_WINDOWS) - 1
CONV_B_WIDTH = 31
CONV_C_WIDTH = 3
DILATED_CFG = ((128, 1), (512, 4), (2048, 16))
ATT_BLK = 128
VMEM_LIMIT = 60 * 1024 * 1024

F32 = jnp.float32
BF16 = jnp.bfloat16


def _cparams(n_axes):
    return pltpu.CompilerParams(dimension_semantics=("arbitrary",) * n_axes,
                                vmem_limit_bytes=VMEM_LIMIT)


def _lane_fold(v):
    n = v.shape[-1] // LANES
    out = v[:, 0:LANES]
    for k in range(1, n):
        out = out + v[:, k * LANES:(k + 1) * LANES]
    return out


def _row_rms_scale(ssq, d_model):
    return lax.rsqrt(jnp.sum(ssq, axis=-1, keepdims=True) * (1.0 / d_model) + EPS)


def _prep_kernel(x_ref, g_ref, x_out, xg_ref, ssq_ref):
    x = x_ref[...]
    x_out[...] = x
    xg_ref[...] = (x * g_ref[...]).astype(BF16)
    ssq_ref[...] = _lane_fold(x * x)


def _prep(x, g_row, tm):
    m, d = x.shape
    return pl.pallas_call(
        _prep_kernel,
        grid=(m // tm,),
        in_specs=[pl.BlockSpec((tm, d), lambda i: (i, 0)),
                  pl.BlockSpec((1, d), lambda i: (0, 0))],
        out_specs=[pl.BlockSpec((tm, d), lambda i: (i, 0)),
                   pl.BlockSpec((tm, d), lambda i: (i, 0)),
                   pl.BlockSpec((tm, LANES), lambda i: (i, 0))],
        out_shape=[jax.ShapeDtypeStruct((m, d), F32),
                   jax.ShapeDtypeStruct((m, d), BF16),
                   jax.ShapeDtypeStruct((m, LANES), F32)],
        compiler_params=_cparams(1),
        name="norm_prep",
    )(x, g_row)


def _side_col(nj):
    return lambda i, j: jnp.where(i == 0, j, nj - 1)


def _up_kernel(l_ref, xg_ref, ssq_ref, xgs_ref, ssqs_ref, *rest, n_w, d_model):
    w_refs, (o_ref, os_ref, lhs_s) = rest[:n_w], rest[n_w:]
    tm = xg_ref.shape[0]
    i, j = pl.program_id(0), pl.program_id(1)

    def weights():
        return [w[...].astype(BF16) for w in w_refs]

    def act(outs):
        return outs[0] if n_w == 1 else jax.nn.silu(outs[0]) * outs[1]

    @pl.when((i == 0) & (j == 0))
    def _():
        lhs_s[0:tm, :] = xg_ref[...]
        lhs_s[tm:, :] = xgs_ref[...]

    @pl.when(i == 0)
    def _():
        r = _row_rms_scale(ssq_ref[...], d_model)
        rs = _row_rms_scale(ssqs_ref[...], d_model)
        lhs = lhs_s[...]
        outs = [jnp.dot(lhs, w, preferred_element_type=F32) for w in weights()]
        o_ref[...] = act([o[0:tm] * r for o in outs]).astype(o_ref.dtype)
        os_ref[...] = act([o[tm:] * rs for o in outs]).astype(os_ref.dtype)

    @pl.when(i > 0)
    def _():
        r = _row_rms_scale(ssq_ref[...], d_model)
        xg = xg_ref[...]
        o_ref[...] = act([jnp.dot(xg, w, preferred_element_type=F32) * r for w in weights()]).astype(o_ref.dtype)


def _normed_matmul(lidx, xg, ssq, xgs, ssqs, ws, *, tm, tn, out_dtype, name):
    m, k = xg.shape
    ms = xgs.shape[0]
    n = ws[0].shape[-1]
    sj = _side_col(n // tn)
    grid_spec = pltpu.PrefetchScalarGridSpec(
        num_scalar_prefetch=1,
        grid=(m // tm, n // tn),
        in_specs=[pl.BlockSpec((tm, k), lambda i, j, l: (i, 0)),
                  pl.BlockSpec((tm, LANES), lambda i, j, l: (i, 0)),
                  pl.BlockSpec((ms, k), lambda i, j, l: (0, 0)),
                  pl.BlockSpec((ms, LANES), lambda i, j, l: (0, 0))]
                 + [pl.BlockSpec((None, k, tn), lambda i, j, l: (l[0], 0, j)) for _ in ws],
        out_specs=[pl.BlockSpec((tm, tn), lambda i, j, l: (i, j)),
                   pl.BlockSpec((ms, tn), lambda i, j, l: (0, sj(i, j)))],
        scratch_shapes=[pltpu.VMEM((tm + ms, k), BF16)],
    )
    return pl.pallas_call(
        functools.partial(_up_kernel, n_w=len(ws), d_model=k),
        grid_spec=grid_spec,
        out_shape=[jax.ShapeDtypeStruct((m, n), out_dtype),
                   jax.ShapeDtypeStruct((ms, n), out_dtype)],
        compiler_params=_cparams(2),
        name=name,
    )(lidx, xg, ssq, xgs, ssqs, *ws)


def _down_kernel(l_ref, *refs, n_a, scale, stacked):
    a_refs, as_refs, w_refs = refs[:n_a], refs[n_a:2 * n_a], refs[2 * n_a:3 * n_a]
    res_ref, ress_ref, g_ref, x_out, xg_out, ssq_out, xs_out, xgs_out, ssqs_out = refs[3 * n_a:3 * n_a + 9]
    lhs_refs = refs[3 * n_a + 9:]
    tm = res_ref.shape[0]
    i, j = pl.program_id(0), pl.program_id(1)
    first_col = j == 0

    def weights():
        return [w[...].astype(BF16) for w in w_refs]

    def matmul(lhs_group, wb):
        acc = None
        for a, w in zip(lhs_group, wb):
            part = jnp.dot(a[...], w, preferred_element_type=F32)
            acc = part if acc is None else acc + part
        return acc

    def finish(acc, res, x_o, xg_o, ssq_o):
        xn = res[...] + scale * acc
        x_o[...] = xn
        xg_o[...] = (xn * g_ref[...]).astype(BF16)
        part_ssq = _lane_fold(xn * xn)

        @pl.when(first_col)
        def _():
            ssq_o[...] = part_ssq

        @pl.when(jnp.logical_not(first_col))
        def _():
            ssq_o[...] += part_ssq

    if not stacked:
        wb = weights()
        finish(matmul(a_refs, wb), res_ref, x_out, xg_out, ssq_out)

        @pl.when(i == 0)
        def _():
            finish(matmul(as_refs, wb), ress_ref, xs_out, xgs_out, ssqs_out)
        return

    @pl.when((i == 0) & first_col)
    def _():
        for lhs, a, a_s in zip(lhs_refs, a_refs, as_refs):
            lhs[0:tm, :] = a[...]
            lhs[tm:, :] = a_s[...]

    @pl.when(i == 0)
    def _():
        acc = matmul(lhs_refs, weights())
        finish(acc[0:tm], res_ref, x_out, xg_out, ssq_out)
        finish(acc[tm:], ress_ref, xs_out, xgs_out, ssqs_out)

    @pl.when(i > 0)
    def _():
        finish(matmul(a_refs, weights()), res_ref, x_out, xg_out, ssq_out)


def _residual_matmul(lidx, a_list, as_list, w, row_blocks, res, ress, g_next, *, scale, tm, tn, stacked, name):
    m, ms = res.shape[0], ress.shape[0]
    n = w.shape[-1]
    sj = _side_col(n // tn)
    in_specs = [pl.BlockSpec((tm, a.shape[1]), lambda i, j, l: (i, 0), pipeline_mode=pl.Buffered(1))
                for a in a_list]
    in_specs += [pl.BlockSpec((ms, a.shape[1]), lambda i, j, l: (0, 0)) for a in as_list]
    in_specs += [pl.BlockSpec((None, a.shape[1], tn), functools.partial(lambda i, j, l, rb: (l[0], rb, j), rb=rb))
                 for a, rb in zip(a_list, row_blocks)]
    in_specs += [pl.BlockSpec((tm, tn), lambda i, j, l: (i, j)),
                 pl.BlockSpec((ms, tn), lambda i, j, l: (0, sj(i, j))),
                 pl.BlockSpec((None, 1, tn), lambda i, j, l: (l[1], 0, j))]
    grid_spec = pltpu.PrefetchScalarGridSpec(
        num_scalar_prefetch=1,
        grid=(m // tm, n // tn),
        in_specs=in_specs,
        out_specs=[pl.BlockSpec((tm, tn), lambda i, j, l: (i, j)),
                   pl.BlockSpec((tm, tn), lambda i, j, l: (i, j)),
                   pl.BlockSpec((tm, LANES), lambda i, j, l: (i, 0)),
                   pl.BlockSpec((ms, tn), lambda i, j, l: (0, sj(i, j))),
                   pl.BlockSpec((ms, tn), lambda i, j, l: (0, sj(i, j))),
                   pl.BlockSpec((ms, LANES), lambda i, j, l: (0, 0))],
        scratch_shapes=[pltpu.VMEM((tm + ms, a.shape[1]), BF16) for a in a_list] if stacked else [],
    )
    return pl.pallas_call(
        functools.partial(_down_kernel, n_a=len(a_list), scale=scale, stacked=stacked),
        grid_spec=grid_spec,
        out_shape=[jax.ShapeDtypeStruct((m, n), F32),
                   jax.ShapeDtypeStruct((m, n), BF16),
                   jax.ShapeDtypeStruct((m, LANES), F32),
                   jax.ShapeDtypeStruct((ms, n), F32),
                   jax.ShapeDtypeStruct((ms, n), BF16),
                   jax.ShapeDtypeStruct((ms, LANES), F32)],
        compiler_params=_cparams(2),
        name=name,
    )(lidx, *a_list, *as_list, *([w] * len(a_list)), res, ress, g_next)


HIST_A = 16
HIST_B = 32
HIST_C = 8


def _mix_abc_kernel(l_ref, *refs, tt, tr, t_valid, n_prev, n_chunks, has_state, dg):
    if has_state:
        (z_ref, sp_ref, sb_ref, sc_ref, wpool_ref, pscale_ref, wdwb_ref, bdwb_ref, lng_ref, lnb_ref, wdwc_ref,
         y_ref, np_ref, nb_ref, nc_ref, full_a, full_b, full_c, shift_b, pooled_s) = refs
    else:
        (z_ref, wpool_ref, pscale_ref, wdwb_ref, bdwb_ref, lng_ref, lnb_ref, wdwc_ref,
         y_ref, np_ref, nb_ref, nc_ref, full_a, full_b, full_c, shift_b, pooled_s) = refs
    c = pl.program_id(1)

    @pl.when(c == 0)
    def _():
        full_a[0:HIST_A, :] = jnp.zeros((HIST_A, dg), F32)
        full_b[0:HIST_B, :] = jnp.zeros((HIST_B, dg), F32)
        full_c[0:HIST_C, :] = jnp.zeros((HIST_C, dg), F32)
        if has_state:
            full_a[HIST_A - POOL_BUF:HIST_A, :] = sp_ref[...]
            full_b[HIST_B - (CONV_B_WIDTH - 1):HIST_B, :] = sb_ref[...]
            full_c[HIST_C - (CONV_C_WIDTH - 1):HIST_C, :] = sc_ref[...]

    full_a[HIST_A:HIST_A + tt, :] = z_ref[:, 0:dg]
    full_b[HIST_B:HIST_B + tt, :] = z_ref[:, dg:2 * dg] * jax.nn.sigmoid(z_ref[:, 2 * dg:3 * dg])
    full_c[HIST_C:HIST_C + tt, :] = z_ref[:, 4 * dg:5 * dg] * z_ref[:, 5 * dg:6 * dg]

    n_shift = HIST_B + tt - SUBLANES
    for s in range(1, SUBLANES):
        shift_b[s - 1, :, :] = full_b[s:s + n_shift, :]

    def tap(w_ref, j):
        w = w_ref[j]
        return w if tr == SUBLANES else jnp.concatenate([w] * (tr // SUBLANES), axis=0)

    pch = dg // len(POOL_WINDOWS)
    for r0 in range(0, tt, tr):
        t_glob = c * tt + r0 + lax.broadcasted_iota(jnp.int32, (tr, 1), 0)
        for g, w in enumerate(POOL_WINDOWS):
            cols = slice(g * pch, (g + 1) * pch)
            cur = full_a[HIST_A + r0:HIST_A + r0 + tr, cols]
            acc = cur
            for s in range(1, w):
                acc = acc + full_a[HIST_A + r0 - s:HIST_A + r0 - s + tr, cols]
            cnt = jnp.minimum(w, n_prev + t_glob + 1).astype(F32)
            pooled_s[r0:r0 + tr, cols] = acc / cnt - cur

        acc = jnp.zeros((tr, dg), F32) + bdwb_ref[...]
        for j in range(CONV_B_WIDTH):
            off = HIST_B - (CONV_B_WIDTH - 1) + j
            base, s = r0 + off - off % SUBLANES, off % SUBLANES
            src = full_b[base:base + tr, :] if s == 0 else shift_b[s - 1, base:base + tr, :]
            acc = acc + tap(wdwb_ref, j) * src
        mu = jnp.mean(acc, axis=-1, keepdims=True)
        xc = acc - mu
        var = jnp.mean(xc * xc, axis=-1, keepdims=True)
        yn = xc * lax.rsqrt(var + EPS) * lng_ref[...] + lnb_ref[...]
        y_ref[r0:r0 + tr, dg:2 * dg] = jax.nn.silu(yn).astype(y_ref.dtype)

        acc = jnp.zeros((tr, dg), F32)
        off = HIST_C - (CONV_C_WIDTH - 1) + r0
        for j in range(CONV_C_WIDTH):
            acc = acc + tap(wdwc_ref, j) * full_c[off + j:off + j + tr, :]
        y_ref[r0:r0 + tr, 2 * dg:3 * dg] = (z_ref[r0:r0 + tr, 3 * dg:4 * dg] * acc).astype(y_ref.dtype)

    for g in range(len(POOL_WINDOWS)):
        cols = slice(g * pch, (g + 1) * pch)
        ya = jnp.dot(pooled_s[:, cols].astype(BF16), wpool_ref[g].astype(BF16), preferred_element_type=F32)
        y_ref[:, cols] = (ya * pscale_ref[:, cols]).astype(y_ref.dtype)

    @pl.when(c == n_chunks - 1)
    def _():
        np_ref[...] = full_a[HIST_A + t_valid - POOL_BUF:HIST_A + t_valid, :]
        nb_ref[...] = full_b[HIST_B + t_valid - (CONV_B_WIDTH - 1):HIST_B + t_valid, :]
        nc_ref[...] = full_c[HIST_C + t_valid - (CONV_C_WIDTH - 1):HIST_C + t_valid, :]

    if n_chunks > 1:
        @pl.when(c < n_chunks - 1)
        def _():
            full_a[0:HIST_A, :] = full_a[tt:tt + HIST_A, :]
            full_b[0:HIST_B, :] = full_b[tt:tt + HIST_B, :]
            full_c[0:HIST_C, :] = full_c[tt:tt + HIST_C, :]


def _mix_abc(lidx, z, states, wts, *, tt, tr, t_valid, n_prev, out_dtype, name):
    b, t, _ = z.shape
    w_pool, pool_scale, w_dw_b, b_dw_b, ln_g, ln_b, w_dw_c = wts
    dg = pool_scale.shape[-1]
    n_chunks = t // tt
    has_state = states is not None

    def lsel(*tail):
        return lambda bi, ci, l: (l[0],) + tail

    in_specs = [pl.BlockSpec((None, tt, 6 * dg), lambda bi, ci, l: (bi, ci, 0))]
    args = [z]
    if has_state:
        for s in states:
            in_specs.append(pl.BlockSpec((None, None) + s.shape[2:], lambda bi, ci, l: (l[0], bi, 0, 0)))
            args.append(s)
    in_specs += [
        pl.BlockSpec((None,) + w_pool.shape[1:], lsel(0, 0, 0)),
        pl.BlockSpec((None, 1, dg), lsel(0, 0)),
        pl.BlockSpec((None, CONV_B_WIDTH, SUBLANES, dg), lsel(0, 0, 0)),
        pl.BlockSpec((None, 1, dg), lsel(0, 0)),
        pl.BlockSpec((None, 1, dg), lsel(0, 0)),
        pl.BlockSpec((None, 1, dg), lsel(0, 0)),
        pl.BlockSpec((None, CONV_C_WIDTH, SUBLANES, dg), lsel(0, 0, 0)),
    ]
    args += [w_pool, pool_scale, w_dw_b, b_dw_b, ln_g, ln_b, w_dw_c]
    grid_spec = pltpu.PrefetchScalarGridSpec(
        num_scalar_prefetch=1,
        grid=(b, n_chunks),
        in_specs=in_specs,
        out_specs=[pl.BlockSpec((None, tt, 3 * dg), lambda bi, ci, l: (bi, ci, 0)),
                   pl.BlockSpec((None, POOL_BUF, dg), lambda bi, ci, l: (bi, 0, 0)),
                   pl.BlockSpec((None, CONV_B_WIDTH - 1, dg), lambda bi, ci, l: (bi, 0, 0)),
                   pl.BlockSpec((None, CONV_C_WIDTH - 1, dg), lambda bi, ci, l: (bi, 0, 0))],
        scratch_shapes=[pltpu.VMEM((HIST_A + tt, dg), F32),
                        pltpu.VMEM((HIST_B + tt, dg), F32),
                        pltpu.VMEM((HIST_C + tt, dg), F32),
                        pltpu.VMEM((SUBLANES - 1, HIST_B + tt - SUBLANES, dg), F32),
                        pltpu.VMEM((tt, dg), F32)],
    )
    return pl.pallas_call(
        functools.partial(_mix_abc_kernel, tt=tt, tr=tr, t_valid=t_valid, n_prev=n_prev,
                          n_chunks=n_chunks, has_state=has_state, dg=dg),
        grid_spec=grid_spec,
        out_shape=[jax.ShapeDtypeStruct((b, t, 3 * dg), out_dtype),
                   jax.ShapeDtypeStruct((b, POOL_BUF, dg), F32),
                   jax.ShapeDtypeStruct((b, CONV_B_WIDTH - 1, dg), F32),
                   jax.ShapeDtypeStruct((b, CONV_C_WIDTH - 1, dg), F32)],
        compiler_params=_cparams(2),
        name=name,
    )(lidx, *args)


def _head_rmsnorm(x, g):
    return x * lax.rsqrt(jnp.mean(x * x, axis=-1, keepdims=True) + EPS) * g


def _dot_nt(a, b):
    return lax.dot_general(a, b, (((1,), (1,)), ((), ())), preferred_element_type=F32)


def _attn_prompt_kernel(l_ref, q_ref, k_ref, v_ref, gq_ref, gk_ref, kstack_ref, vstack_ref, y_ref, ko_ref, vo_ref,
                        qn_s, kn_s, m_s, l_s, acc_s, qf_s, kf_s, vf_s, mf_s, lf_s, accf_s,
                        kb_s, vb_s, kfb_s, vfb_s, *, n_heads):
    t = q_ref.shape[0]
    blk = ATT_BLK
    (w0, d0), (w1, d1), (w2, d2) = DILATED_CFG
    assert d0 == 1 and d2 % d1 == 0 and w0 // d0 == blk and w1 // d1 == blk and w2 // d2 == blk
    assert (t // d2) % blk == 0
    fold, inner, tf = d1, d2 // d1, t // d1

    qn_s[...] = _head_rmsnorm(q_ref[...], gq_ref[...]) * (HEAD_DIM ** -0.5)
    kn_s[...] = _head_rmsnorm(k_ref[...], gk_ref[...])
    head_rows = pl.ds(pl.program_id(1), t, stride=n_heads)
    ko_ref[head_rows, :] = kn_s[...]
    vo_ref[head_rows, :] = v_ref[...]

    row1 = lax.broadcasted_iota(jnp.int32, (blk, blk), 0)
    col1 = lax.broadcasted_iota(jnp.int32, (blk, blk), 1)
    causal_mask = col1 <= row1
    row2 = lax.broadcasted_iota(jnp.int32, (blk, 2 * blk), 0)
    col2 = lax.broadcasted_iota(jnp.int32, (blk, 2 * blk), 1)
    band_mask = (col2 >= row2) & (col2 <= row2 + blk)

    def rows(start, n, stride):
        return pl.ds(start, n, stride=stride) if stride > 1 else pl.ds(start, n)

    def update(qkv, state, start, stride, with_prev, first_branch, old=None):
        q_r, k_r, v_r, kb_r, vb_r = qkv
        m_r, l_r, acc_r = state
        qidx = rows(start, blk, stride)
        kidx = rows(start - blk * stride, 2 * blk, stride) if with_prev else qidx
        qb = q_r[qidx, :].astype(BF16)
        kb = kb_r[kidx, :] if stride == 1 else k_r[kidx, :].astype(BF16)
        s = _dot_nt(qb, kb)
        s = jnp.where(band_mask if with_prev else causal_mask, s, NEG)
        mx = jnp.max(s, axis=-1, keepdims=True)
        vb = vb_r[kidx, :] if stride == 1 else v_r[kidx, :].astype(BF16)
        if first_branch:
            p = jnp.exp(s - mx)
            m_r[qidx, :] = jnp.broadcast_to(mx, (blk, LANES))
            l_r[qidx, :] = jnp.broadcast_to(jnp.sum(p, axis=-1, keepdims=True), (blk, LANES))
            acc_r[qidx, :] = jnp.dot(p.astype(BF16), vb, preferred_element_type=F32)
        else:
            (mo_r, lo_r, acco_r), oidx = (state, qidx) if old is None else old
            m_old = mo_r[oidx, :]
            m_new = jnp.maximum(m_old, mx)
            alpha = jnp.exp(m_old - m_new)
            p = jnp.exp(s - (jnp.concatenate([m_new, m_new], axis=-1) if with_prev else m_new))
            m_r[qidx, :] = m_new
            l_r[qidx, :] = alpha * lo_r[oidx, :] + jnp.sum(p, axis=-1, keepdims=True)
            acc_r[qidx, :] = alpha * acco_r[oidx, :] + jnp.dot(p.astype(BF16), vb, preferred_element_type=F32)

    natural = ((qn_s, kn_s, v_ref, kb_s, vb_s), (m_s, l_s, acc_s))
    folded = ((qf_s, kf_s, vf_s, kfb_s, vfb_s), (mf_s, lf_s, accf_s))

    kb_s[...] = kn_s[...].astype(BF16)
    vb_s[...] = v_ref[...].astype(BF16)
    for i in range(t // blk):
        update(*natural, i * blk, 1, i > 0, True)

    for src, dst in zip(natural[0][:3], folded[0][:3]):
        for r in range(fold):
            dst[r * tf:(r + 1) * tf, :] = src[pl.ds(r, tf, stride=fold), :]
    kfb_s[...] = kf_s[...].astype(BF16)
    vfb_s[...] = vf_s[...].astype(BF16)

    for r in range(fold):
        for i in range(tf // blk):
            update(*folded, r * tf + i * blk, 1, i > 0, False,
                   old=(natural[1], pl.ds(r + i * blk * fold, blk, stride=fold)))

    for r in range(fold):
        for r2 in range(inner):
            for i in range(tf // inner // blk):
                update(*folded, r * tf + r2 + i * blk * inner, inner, i > 0, False)

    accf_s[...] = accf_s[...] / lf_s[...]
    for r in range(fold):
        acc_s[pl.ds(r, tf, stride=fold), :] = accf_s[r * tf:(r + 1) * tf, :]
    y_ref[...] = acc_s[...].astype(y_ref.dtype)


def _attn_prompt(lidx, z, gq, gk, k_stack, v_stack, *, n_heads):
    b, t, n_in = z.shape
    dd = n_heads * HEAD_DIM
    q0 = (n_in - 3 * dd) // HEAD_DIM
    hd = HEAD_DIM
    window_out = pl.BlockSpec((None, None, t * n_heads, hd), lambda bi, h, l: (l[0], bi, 0, 0))
    grid_spec = pltpu.PrefetchScalarGridSpec(
        num_scalar_prefetch=1,
        grid=(b, n_heads),
        in_specs=[pl.BlockSpec((None, t, hd), lambda bi, h, l: (bi, 0, q0 + h)),
                  pl.BlockSpec((None, t, hd), lambda bi, h, l: (bi, 0, q0 + n_heads + h)),
                  pl.BlockSpec((None, t, hd), lambda bi, h, l: (bi, 0, q0 + 2 * n_heads + h)),
                  pl.BlockSpec((None, 1, hd), lambda bi, h, l: (l[0], 0, 0)),
                  pl.BlockSpec((None, 1, hd), lambda bi, h, l: (l[0], 0, 0)),
                  pl.BlockSpec(memory_space=pl.ANY),
                  pl.BlockSpec(memory_space=pl.ANY)],
        out_specs=[pl.BlockSpec((None, t, hd), lambda bi, h, l: (bi, 0, h)), window_out, window_out],
        scratch_shapes=[pltpu.VMEM((t, hd), F32)] * 11 + [pltpu.VMEM((t, hd), BF16)] * 4,
    )
    return pl.pallas_call(
        functools.partial(_attn_prompt_kernel, n_heads=n_heads),
        grid_spec=grid_spec,
        out_shape=[jax.ShapeDtypeStruct((b, t, dd), BF16),
                   jax.ShapeDtypeStruct(k_stack.shape, F32),
                   jax.ShapeDtypeStruct(v_stack.shape, F32)],
        input_output_aliases={6: 1, 7: 2},
        compiler_params=_cparams(2),
        name="attn_prompt",
    )(lidx, z, z, z, gq, gk, k_stack, v_stack)


def _attn_sample_kernel(l_ref, qkv_ref, kc_ref, vc_ref, gq_ref, gk_ref, y_ref, ko_ref, *, t_valid, n_heads):
    tq = qkv_ref.shape[0]
    hd = HEAD_DIM
    lc = kc_ref.shape[0] // n_heads

    def multiplicity(dist):
        w = jnp.zeros(dist.shape, F32)
        for window, dil in DILATED_CFG:
            hit = (dist >= 0) & (dist <= window) & ((dist & (dil - 1)) == 0)
            w = w + hit.astype(F32)
        return w

    d1 = lc + lax.broadcasted_iota(jnp.int32, (tq, lc), 0) - lax.broadcasted_iota(jnp.int32, (tq, lc), 1)
    w1 = multiplicity(d1)
    d2 = lax.broadcasted_iota(jnp.int32, (tq, tq), 0) - lax.broadcasted_iota(jnp.int32, (tq, tq), 1)
    w2 = multiplicity(d2) * (lax.broadcasted_iota(jnp.int32, (tq, tq), 1) < t_valid).astype(F32)

    for h in range(n_heads):
        q = qkv_ref[:, h * hd:(h + 1) * hd]
        k = qkv_ref[:, (n_heads + h) * hd:(n_heads + h + 1) * hd]
        v = qkv_ref[:, (2 * n_heads + h) * hd:(2 * n_heads + h + 1) * hd]
        qn = (_head_rmsnorm(q, gq_ref[...]) * (HEAD_DIM ** -0.5)).astype(BF16)
        kn = _head_rmsnorm(k, gk_ref[...])
        ko_ref[:, h * hd:(h + 1) * hd] = kn
        head_rows = pl.ds(h, lc, stride=n_heads)
        s1 = jnp.where(w1 > 0, _dot_nt(qn, kc_ref[head_rows, :].astype(BF16)), NEG)
        s2 = jnp.where(w2 > 0, _dot_nt(qn, kn.astype(BF16)), NEG)
        m = jnp.maximum(jnp.max(s1, axis=-1, keepdims=True), jnp.max(s2, axis=-1, keepdims=True))
        p1 = w1 * jnp.exp(s1 - m)
        p2 = w2 * jnp.exp(s2 - m)
        den = jnp.sum(p1, axis=-1, keepdims=True) + jnp.sum(p2, axis=-1, keepdims=True)
        num = (jnp.dot(p1.astype(BF16), vc_ref[head_rows, :].astype(BF16), preferred_element_type=F32)
               + jnp.dot(p2.astype(BF16), v.astype(BF16), preferred_element_type=F32))
        y_ref[:, h * hd:(h + 1) * hd] = (num / den).astype(y_ref.dtype)


def _attn_sample(lidx, z, cache_k, cache_v, gq, gk, *, n_heads, t_valid):
    b, tq, n_in = z.shape
    depth, _, lc = cache_k.shape[:3]
    hd = HEAD_DIM
    dd = n_heads * hd
    assert (n_in - 3 * dd) % (3 * dd) == 0
    qkv_block = (n_in - 3 * dd) // (3 * dd)
    for window, dil in DILATED_CFG:
        assert dil & (dil - 1) == 0 and window <= lc
    ck = cache_k.reshape(depth, b, lc * n_heads, hd)
    cv = cache_v.reshape(depth, b, lc * n_heads, hd)
    grid_spec = pltpu.PrefetchScalarGridSpec(
        num_scalar_prefetch=1,
        grid=(b,),
        in_specs=[pl.BlockSpec((None, tq, 3 * dd), lambda bi, l: (bi, 0, qkv_block)),
                  pl.BlockSpec((None, None, lc * n_heads, hd), lambda bi, l: (l[0], bi, 0, 0)),
                  pl.BlockSpec((None, None, lc * n_heads, hd), lambda bi, l: (l[0], bi, 0, 0)),
                  pl.BlockSpec((None, 1, hd), lambda bi, l: (l[0], 0, 0)),
                  pl.BlockSpec((None, 1, hd), lambda bi, l: (l[0], 0, 0))],
        out_specs=[pl.BlockSpec((None, tq, dd), lambda bi, l: (bi, 0, 0)),
                   pl.BlockSpec((None, tq, dd), lambda bi, l: (bi, 0, 0))],
    )
    return pl.pallas_call(
        functools.partial(_attn_sample_kernel, t_valid=t_valid, n_heads=n_heads),
        grid_spec=grid_spec,
        out_shape=[jax.ShapeDtypeStruct((b, tq, dd), F32),
                   jax.ShapeDtypeStruct((b, tq, dd), F32)],
        compiler_params=_cparams(1),
        name="attn_sample",
    )(lidx, z, ck, cv, gq, gk)


SAMPLE_ROWS = 8
TM = 1024
TN_UP = 256
TN_WIDE = 512


def kernel(x_prompt, x_sample, state_pool, state_conv_b, state_conv_c, cache_k, cache_v, g_ffn1, w_ffn1_gate, w_ffn1_up, w_ffn1_down, g_mix, w_in, w_pool, pool_scale, w_dw_b, b_dw_b, ln_b_g, ln_b_b, w_dw_c, q_norm_g, k_norm_g, w_out, g_ffn2, w_ffn2_gate, w_ffn2_up, w_ffn2_down):
    bp, tp, d = x_prompt.shape
    bs, ts, _ = x_sample.shape
    depth = g_ffn1.shape[0]
    dg = pool_scale.shape[-1]
    n_heads = cache_k.shape[3]
    n_in = w_in.shape[-1]
    assert ts <= SAMPLE_ROWS and state_pool.shape[2] == POOL_BUF

    row = lambda a: a.reshape(depth, 1, a.shape[-1])
    g1, gm, g2 = row(g_ffn1), row(g_mix), row(g_ffn2)
    taps = lambda a: jnp.broadcast_to(a[:, :, None, :], a.shape[:2] + (SUBLANES, a.shape[-1]))
    mix_w = (w_pool, row(pool_scale), taps(w_dw_b), row(b_dw_b), row(ln_b_g), row(ln_b_b), taps(w_dw_c))
    gq, gk = row(q_norm_g), row(k_norm_g)

    xp, xgp, ssqp = _prep(x_prompt.reshape(bp * tp, d), g1[0], 256)
    xs, xgs, ssqs = _prep(x_sample.reshape(bs * ts, d), g1[0], bs * ts)
    k_stack = lax.empty((depth, bp, tp * n_heads, HEAD_DIM), F32)
    v_stack = lax.empty((depth, bp, tp * n_heads, HEAD_DIM), F32)

    def ffn(lidx, st, wg, wu, wd, g_next, tag):
        xp, xgp, ssqp, xs, xgs, ssqs = st
        ap, a_s = _normed_matmul(lidx, xgp, ssqp, xgs, ssqs, (wg, wu), tm=TM, tn=TN_UP, out_dtype=BF16,
                                 name="ffn_up" + tag)
        return tuple(_residual_matmul(lidx, [ap], [a_s], wd, [0], xp, xs, g_next, scale=0.5, tm=TM, tn=TN_UP,
                                      stacked=False, name="ffn_down" + tag))

    def layer_step(carry, l):
        st, k_stack, v_stack = carry
        lnext = jnp.minimum(l + 1, depth - 1)
        l_same = jnp.stack([l, l]).astype(jnp.int32)
        l_next = jnp.stack([l, lnext]).astype(jnp.int32)

        st = ffn(l_same, st, w_ffn1_gate, w_ffn1_up, w_ffn1_down, gm, "1")
        xp, xgp, ssqp, xs, xgs, ssqs = st
        zp, zs = _normed_matmul(l_same, xgp, ssqp, xgs, ssqs, (w_in,), tm=TM, tn=TN_WIDE, out_dtype=F32,
                                name="proj_in")

        zp = zp.reshape(bp, tp, n_in)
        yabc, pool_p, convb_p, convc_p = _mix_abc(
            l_same, zp, None, mix_w, tt=128, tr=16, t_valid=128, n_prev=0, out_dtype=BF16, name="mix_abc_p")
        yd, k_stack, v_stack = _attn_prompt(l_same, zp, gq, gk, k_stack, v_stack, n_heads=n_heads)

        zs = jnp.pad(zs.reshape(bs, ts, n_in), ((0, 0), (0, SAMPLE_ROWS - ts), (0, 0)))
        yabc_s, pool_s, convb_s, convc_s = _mix_abc(
            l_same, zs, (state_pool, state_conv_b, state_conv_c), mix_w, tt=SAMPLE_ROWS, tr=SAMPLE_ROWS,
            t_valid=ts, n_prev=POOL_BUF, out_dtype=F32, name="mix_abc_s")
        yd_s, k_s = _attn_sample(l_same, zs, cache_k, cache_v, gq, gk, n_heads=n_heads, t_valid=ts)
        v_s = zs[:, :ts, n_in - dg:]

        a_p = [yabc.reshape(bp * tp, 3 * dg), yd.reshape(bp * tp, dg)]
        a_s = [yabc_s[:, :ts].reshape(bs * ts, 3 * dg).astype(BF16), yd_s[:, :ts].reshape(bs * ts, dg).astype(BF16)]
        st = tuple(_residual_matmul(l_same, a_p, a_s, w_out, [0, 3], xp, xs, g2, scale=1.0, tm=TM, tn=TN_WIDE,
                                    stacked=True, name="proj_out"))
        st = ffn(l_next, st, w_ffn2_gate, w_ffn2_up, w_ffn2_down, g1, "2")

        outs = (pool_p, pool_s, convb_p, convb_s, convc_p, convc_s,
                k_s[:, :ts].reshape(bs, ts, n_heads, HEAD_DIM), v_s.reshape(bs, ts, n_heads, HEAD_DIM))
        return (st, k_stack, v_stack), outs

    init = ((xp, xgp, ssqp, xs, xgs, ssqs), k_stack, v_stack)
    (st, k_stack, v_stack), outs = lax.scan(layer_step, init, jnp.arange(depth, dtype=jnp.int32))
    window = lambda a: a.reshape(depth, bp, tp, n_heads, HEAD_DIM)
    return ((st[0].reshape(bp, tp, d), st[3].reshape(bs, ts, d)) + tuple(outs[:6])
            + (window(k_stack), window(v_stack)) + tuple(outs[6:]))
```
